```python
import jax, jax.numpy as jnp
from jax import lax
import numpy as np

D_MODEL = 1024
BATCH = 1
SEQ = 16384
DEPTH = 4
DEC_BATCH = 2
DEC_SEQ = 8192
PAST_LEN = 128

GRID_W = 64
NA_HEADS = 16
HEAD_DIM = D_MODEL // NA_HEADS
MAX_KH = 8
KW = 16
N_MIXERS = 2
CHUNK = 128
SG_GROUPS = 16
SG_WIDTH = D_MODEL
SG_GROUP_DIM = SG_WIDTH // SG_GROUPS
D_FF = ((8 * D_MODEL + 3 * 256 - 1) // (3 * 256)) * 256
N_NA_LAYERS = (DEPTH + N_MIXERS - 1) // N_MIXERS
N_SG_LAYERS = DEPTH // N_MIXERS
ALPHA = (2 * DEPTH) ** 0.25
BETA = (8 * DEPTH) ** -0.25
LN_EPS = 1e-5

kernel_name = "hybrid_natten_gmlp_deepnorm_encoder"


def layer_norm(x, g, b):
    xf = x.astype(jnp.float32)
    mu = jnp.mean(xf, axis=-1, keepdims=True)
    var = jnp.mean(jnp.square(xf - mu), axis=-1, keepdims=True)
    y = (xf - mu) * lax.rsqrt(var + LN_EPS)
    return (y * g.astype(jnp.float32) + b.astype(jnp.float32)).astype(x.dtype)


def neighbourhood_attention(x, w_in, rpb, w_out):
    B, T, D = x.shape
    rows = T // GRID_W
    kh = min(MAX_KH, rows)
    q, k, v = jnp.split(x @ w_in, 3, axis=-1)
    q = q.reshape(B, rows, GRID_W, NA_HEADS, HEAD_DIM) * (HEAD_DIM ** -0.5)
    k = k.reshape(B, rows, GRID_W, NA_HEADS, HEAD_DIM)
    v = v.reshape(B, rows, GRID_W, NA_HEADS, HEAD_DIM)
    r_ar = np.arange(rows)
    row_start = np.clip(r_ar - kh // 2, 0, rows - kh)
    dr_idx = row_start[:, None] + np.arange(kh)[None, :] - r_ar[:, None] + MAX_KH - 1
    c_ar = np.arange(GRID_W)
    col_start = np.clip(c_ar - KW // 2, 0, GRID_W - KW)
    col_idx = col_start[:, None] + np.arange(KW)[None, :]
    dc_idx = col_idx - c_ar[:, None] + KW - 1
    rpb_c = rpb[:, :, dc_idx]

    def row_fn(args):
        q_r, rs, dr = args
        k_rows = lax.dynamic_slice_in_dim(k, rs, kh, axis=1)
        v_rows = lax.dynamic_slice_in_dim(v, rs, kh, axis=1)
        k_win = k_rows[:, :, col_idx]
        v_win = v_rows[:, :, col_idx]
        bias = jnp.take(rpb_c, dr, axis=1)
        s = jnp.einsum('bchd,bjckhd->bhjck', q_r, k_win).astype(jnp.float32) + bias.astype(jnp.float32)
        p = jax.nn.softmax(s, axis=(2, 4)).astype(v.dtype)
        return jnp.einsum('bhjck,bjckhd->bchd', p, v_win)

    o = lax.map(row_fn, (jnp.moveaxis(q, 1, 0),
                         jnp.asarray(row_start, dtype=jnp.int32),
                         jnp.asarray(dr_idx, dtype=jnp.int32)))
    o = jnp.moveaxis(o, 0, 1).reshape(B, T, D)
    return o @ w_out


def spatial_gating(x, w_in, ln_g, ln_b, w_s, b_s, w_out):
    B, T, D = x.shape
    n_chunks = T // CHUNK
    u, v = jnp.split(jax.nn.gelu(x @ w_in, approximate=False), 2, axis=-1)
    v = layer_norm(v, ln_g, ln_b).reshape(B, n_chunks, CHUNK, SG_GROUPS, SG_GROUP_DIM)
    v = jnp.einsum('gpq,bnqgc->bnpgc', w_s, v) + b_s.T[None, None, :, :, None]
    return (u * v.reshape(B, T, SG_WIDTH)) @ w_out


def swiglu(x, w_in, w_out):
    g, h = jnp.split(x @ w_in, 2, axis=-1)
    return (jax.nn.silu(g) * h) @ w_out


def trunk(x, na_w_in, na_rpb, na_w_out, sg_w_in, sg_ln_g, sg_ln_b, sg_w_s, sg_b_s, sg_w_out,
          ln_mix_g, ln_mix_b, ffn_w_in, ffn_w_out, ln_ffn_g, ln_ffn_b):
    for i in range(DEPTH):
        j = i // N_MIXERS
        if i % N_MIXERS == 0:
            m = neighbourhood_attention(x, na_w_in[j], na_rpb[j], na_w_out[j])
        else:
            m = spatial_gating(x, sg_w_in[j], sg_ln_g[j], sg_ln_b[j], sg_w_s[j], sg_b_s[j], sg_w_out[j])
        x = layer_norm(ALPHA * x + m, ln_mix_g[i], ln_mix_b[i])
        x = layer_norm(ALPHA * x + swiglu(x, ffn_w_in[i], ffn_w_out[i]), ln_ffn_g[i], ln_ffn_b[i])
    return x


def setup_inputs(seed: int = 0) -> dict:
    key = jax.random.key(seed)
    ks = jax.random.split(key, 24)
    f32 = jnp.float32
    nrm = lambda k, shape, s: jax.random.normal(k, shape, f32) * s
    D = D_MODEL
    x_prompt = nrm(ks[0], (BATCH, SEQ, D), 1.0)
    x_sample = nrm(ks[1], (DEC_BATCH, DEC_SEQ, D), 1.0)
    na_w_in = jnp.concatenate([nrm(ks[2], (N_NA_LAYERS, D, 2 * D), D ** -0.5),
                               nrm(ks[3], (N_NA_LAYERS, D, D), BETA * D ** -0.5)], axis=-1)
    na_rpb = nrm(ks[4], (N_NA_LAYERS, NA_HEADS, 2 * MAX_KH - 1, 2 * KW - 1), 0.1)
    na_w_out = nrm(ks[5], (N_NA_LAYERS, D, D), BETA * D ** -0.5)
    sg_w_in = nrm(ks[6], (N_SG_LAYERS, D, 2 * SG_WIDTH), D ** -0.5)
    sg_ln_g = 1.0 + nrm(ks[7], (N_SG_LAYERS, SG_WIDTH), 0.01)
    sg_ln_b = nrm(ks[8], (N_SG_LAYERS, SG_WIDTH), 0.01)
    sg_w_s = nrm(ks[9], (N_SG_LAYERS, SG_GROUPS, CHUNK, CHUNK), CHUNK ** -0.5)
    sg_b_s = 1.0 + nrm(ks[10], (N_SG_LAYERS, SG_GROUPS, CHUNK), 0.1)
    sg_w_out = nrm(ks[11], (N_SG_LAYERS, SG_WIDTH, D), BETA * SG_WIDTH ** -0.5)
    ln_mix_g = 1.0 + nrm(ks[12], (DEPTH, D), 0.01)
    ln_mix_b = nrm(ks[13], (DEPTH, D), 0.01)
    ffn_w_in = nrm(ks[14], (DEPTH, D, 2 * D_FF), D ** -0.5)
    ffn_w_out = nrm(ks[15], (DEPTH, D_FF, D), BETA * D_FF ** -0.5)
    ln_ffn_g = 1.0 + nrm(ks[16], (DEPTH, D), 0.01)
    ln_ffn_b = nrm(ks[17], (DEPTH, D), 0.01)
    return {"x_prompt": x_prompt, "x_sample": x_sample,
            "na_w_in": na_w_in, "na_rpb": na_rpb, "na_w_out": na_w_out,
            "sg_w_in": sg_w_in, "sg_ln_g": sg_ln_g, "sg_ln_b": sg_ln_b,
            "sg_w_s": sg_w_s, "sg_b_s": sg_b_s, "sg_w_out": sg_w_out,
            "ln_mix_g": ln_mix_g, "ln_mix_b": ln_mix_b,
            "ffn_w_in": ffn_w_in, "ffn_w_out": ffn_w_out,
            "ln_ffn_g": ln_ffn_g, "ln_ffn_b": ln_ffn_b}


def reference(x_prompt, x_sample, na_w_in, na_rpb, na_w_out, sg_w_in, sg_ln_g, sg_ln_b,
              sg_w_s, sg_b_s, sg_w_out, ln_mix_g, ln_mix_b, ffn_w_in, ffn_w_out,
              ln_ffn_g, ln_ffn_b):
    y_prompt = trunk(x_prompt, na_w_in, na_rpb, na_w_out, sg_w_in, sg_ln_g, sg_ln_b, sg_w_s,
                     sg_b_s, sg_w_out, ln_mix_g, ln_mix_b, ffn_w_in, ffn_w_out, ln_ffn_g, ln_ffn_b)
    y_sample = trunk(x_sample, na_w_in, na_rpb, na_w_out, sg_w_in, sg_ln_g, sg_ln_b, sg_w_s,
                     sg_b_s, sg_w_out, ln_mix_g, ln_mix_b, ffn_w_in, ffn_w_out, ln_ffn_g, ln_ffn_b)
    return (y_prompt, y_sample)
```

```python
import functools

import numpy as np
import jax
import jax.numpy as jnp
from jax import lax
from jax.experimental import pallas as pl
from jax.experimental.pallas import tpu as pltpu

D_MODEL = 1024
DEPTH = 4
GRID_W = 64
NA_HEADS = 16
HEAD_DIM = D_MODEL // NA_HEADS
MAX_KH = 8
KW = 16
CHUNK = 128
SG_GROUPS = 16
SG_WIDTH = D_MODEL
D_FF = 2816
ALPHA = (2 * DEPTH) ** 0.25
LN_EPS = 1e-5

LANES = 128
HEAD_PAIRS = D_MODEL // LANES
ROW_BLOCK = 8
KV_ROWS = 2 * ROW_BLOCK
TOKEN_TILE = 512
FF_CHUNK = 256
MASK_BIAS = -1e30
MIB = 1024 * 1024

f32 = jnp.float32
bf16 = jnp.bfloat16


def _layer_norm(y, g, b):
    mu = jnp.mean(y, axis=-1, keepdims=True)
    yc = y - mu
    var = jnp.mean(yc * yc, axis=-1, keepdims=True)
    return yc * lax.rsqrt(var + LN_EPS) * g + b


def _gelu(z):
    return 0.5 * z * (1.0 + lax.erf(z * (0.5 ** 0.5)))


def _const_spec(shape):
    zeros = (0,) * len(shape)
    return pl.BlockSpec(shape, lambda *_: zeros, pipeline_mode=pl.Buffered(1))


def _params(vmem_mib, n_grid=1):
    return pltpu.CompilerParams(
        dimension_semantics=("arbitrary",) * n_grid,
        vmem_limit_bytes=vmem_mib * MIB)


def _qkv_kernel(x_ref, w_ref, o_ref):
    xb = x_ref[...].astype(bf16)
    for c in range(3):
        cols = slice(c * D_MODEL, (c + 1) * D_MODEL)
        acc = jnp.dot(xb, w_ref[:, cols], preferred_element_type=f32)
        if c == 0:
            acc = acc * (HEAD_DIM ** -0.5)
        o_ref[:, cols] = acc.astype(bf16)


def _qkv_call(x, w):
    t = x.shape[0]
    return pl.pallas_call(
        _qkv_kernel,
        grid=(t // TOKEN_TILE,),
        in_specs=[pl.BlockSpec((TOKEN_TILE, D_MODEL), lambda i: (i, 0)),
                  _const_spec((D_MODEL, 3 * D_MODEL))],
        out_specs=pl.BlockSpec((TOKEN_TILE, 3 * D_MODEL), lambda i: (i, 0)),
        out_shape=jax.ShapeDtypeStruct((t, 3 * D_MODEL), bf16),
        compiler_params=_params(40),
        name="na_qkv",
    )(x, w)


def _na_kernel(q_ref, k_ref, v_ref, bias_ref, o_ref, *, rows):
    i = pl.program_id(1)
    wstart = jnp.clip(ROW_BLOCK * i - MAX_KH // 2, 0, rows - KV_ROWS)
    lane = lax.broadcasted_iota(jnp.int32, (GRID_W, LANES), 1)
    first_head = lane < HEAD_DIM
    for rr in range(ROW_BLOCK):
        r = ROW_BLOCK * i + rr
        rs = jnp.clip(r - MAX_KH // 2, 0, rows - MAX_KH)
        off = r - rs
        ks = pl.multiple_of((rs - wstart) * GRID_W, GRID_W)
        qrows = slice(rr * GRID_W, (rr + 1) * GRID_W)
        for hp in range(HEAD_PAIRS):
            cols = slice(hp * LANES, (hp + 1) * LANES)
            qp = q_ref[qrows, cols]
            kp = k_ref[pl.ds(ks, MAX_KH * GRID_W), cols]
            vp = v_ref[pl.ds(ks, MAX_KH * GRID_W), cols]
            outs = []
            for hh in range(2):
                sel = first_head if hh == 0 else jnp.logical_not(first_head)
                qm = jnp.where(sel, qp, jnp.zeros_like(qp))
                s = lax.dot_general(qm, kp, (((1,), (1,)), ((), ())),
                                    preferred_element_type=f32)
                s = s + bias_ref[off, 2 * hp + hh]
                m = jnp.max(s, axis=-1, keepdims=True)
                p = jnp.exp(s - m)
                l = jnp.sum(p, axis=-1, keepdims=True)
                pv = jnp.dot(p.astype(bf16), vp, preferred_element_type=f32)
                outs.append(pv / l)
            o_ref[qrows, cols] = jnp.where(first_head, outs[0], outs[1]).astype(bf16)


def _na_call(qkv, bias, n_img, rows):
    t = qkv.shape[0]
    nb = rows // ROW_BLOCK
    blk = ROW_BLOCK * GRID_W
    win = KV_ROWS * GRID_W

    def kv_map(col):
        def index_map(b, i):
            wstart = jnp.clip(ROW_BLOCK * i - MAX_KH // 2, 0, rows - KV_ROWS)
            return ((b * rows + wstart) * GRID_W, col * D_MODEL)
        return index_map

    kv_spec = lambda col: pl.BlockSpec((pl.Element(win), pl.Element(D_MODEL)), kv_map(col))
    return pl.pallas_call(
        functools.partial(_na_kernel, rows=rows),
        grid=(n_img, nb),
        in_specs=[pl.BlockSpec((blk, D_MODEL), lambda b, i: (b * nb + i, 0)),
                  kv_spec(1), kv_spec(2),
                  _const_spec(bias.shape)],
        out_specs=pl.BlockSpec((blk, D_MODEL), lambda b, i: (b * nb + i, 0)),
        out_shape=jax.ShapeDtypeStruct((t, D_MODEL), bf16),
        compiler_params=_params(48, 2),
        name="na_attn",
    )(qkv, qkv, qkv, bias)


def _na_bias_table(rpb):
    c = np.arange(GRID_W)
    col_start = np.clip(c - KW // 2, 0, GRID_W - KW)
    kc = np.arange(GRID_W)
    in_win = (kc[None, :] >= col_start[:, None]) & (kc[None, :] < col_start[:, None] + KW)
    dc = np.clip(kc[None, :] - c[:, None] + KW - 1, 0, 2 * KW - 2)
    off = np.arange(MAX_KH)
    j = np.arange(MAX_KH)
    dr = j[None, :] - off[:, None] + MAX_KH - 1
    t = rpb[:, dr[:, :, None, None], dc[None, None, :, :]]
    t = jnp.where(in_win[None, None, None], t.astype(f32), MASK_BIAS)
    t = jnp.transpose(t, (1, 0, 3, 2, 4))
    return t.reshape(MAX_KH, NA_HEADS, GRID_W, MAX_KH * GRID_W)


def _proj_ln_kernel(x_ref, a_ref, w_ref, g_ref, b_ref, y_ref):
    m = jnp.dot(a_ref[...], w_ref[...], preferred_element_type=f32)
    y_ref[...] = _layer_norm(ALPHA * x_ref[...] + m, g_ref[...], b_ref[...])


def _proj_ln_call(x, a, w, g, b):
    t = x.shape[0]
    tile = lambda: pl.BlockSpec((TOKEN_TILE, D_MODEL), lambda i: (i, 0))
    return pl.pallas_call(
        _proj_ln_kernel,
        grid=(t // TOKEN_TILE,),
        in_specs=[tile(), tile(), _const_spec((D_MODEL, D_MODEL)),
                  _const_spec((1, D_MODEL)), _const_spec((1, D_MODEL))],
        out_specs=tile(),
        out_shape=jax.ShapeDtypeStruct((t, D_MODEL), f32),
        compiler_params=_params(32),
        name="na_out_ln",
    )(x, a, w, g, b)


def _sg_kernel(x_ref, win_ref, lng_ref, lnb_ref, wsp_ref, bsp_ref, wout_ref, g_ref, b_ref,
               y_ref, gated_ref):
    x = x_ref[...]
    xb = x.astype(bf16)
    u = _gelu(jnp.dot(xb, win_ref[:, :SG_WIDTH], preferred_element_type=f32))
    v = _gelu(jnp.dot(xb, win_ref[:, SG_WIDTH:], preferred_element_type=f32))
    vb = _layer_norm(v, lng_ref[...], lnb_ref[...]).astype(bf16)
    lane = lax.broadcasted_iota(jnp.int32, (CHUNK, LANES), 1)
    first_group = lane < SG_WIDTH // SG_GROUPS
    zero = jnp.zeros((CHUNK, LANES), bf16)
    for n in range(TOKEN_TILE // CHUNK):
        toks = slice(n * CHUNK, (n + 1) * CHUNK)
        for j in range(SG_GROUPS // 2):
            cols = slice(j * LANES, (j + 1) * LANES)
            vp = vb[toks, cols]
            rhs = jnp.concatenate([jnp.where(first_group, vp, zero),
                                   jnp.where(first_group, zero, vp)], axis=0)
            sp = jnp.dot(wsp_ref[j], rhs, preferred_element_type=f32) + bsp_ref[j]
            gated_ref[toks, cols] = (u[toks, cols] * sp).astype(bf16)
    m = jnp.dot(gated_ref[...], wout_ref[...], preferred_element_type=f32)
    y_ref[...] = _layer_norm(ALPHA * x + m, g_ref[...], b_ref[...])


def _sg_call(x, win, lng, lnb, wsp, bsp, wout, g, b):
    t = x.shape[0]
    tile = lambda: pl.BlockSpec((TOKEN_TILE, D_MODEL), lambda i: (i, 0))
    return pl.pallas_call(
        _sg_kernel,
        grid=(t // TOKEN_TILE,),
        in_specs=[tile(), _const_spec(win.shape), _const_spec((1, SG_WIDTH)), _const_spec((1, SG_WIDTH)),
                  _const_spec(wsp.shape), _const_spec(bsp.shape), _const_spec(wout.shape),
                  _const_spec((1, D_MODEL)), _const_spec((1, D_MODEL))],
        out_specs=tile(),
        out_shape=jax.ShapeDtypeStruct((t, D_MODEL), f32),
        scratch_shapes=[pltpu.VMEM((TOKEN_TILE, SG_WIDTH), bf16)],
        compiler_params=_params(48),
        name="sg_layer",
    )(x, win, lng, lnb, wsp, bsp, wout, g, b)


def _ffn_kernel(x_ref, win_ref, wout_ref, g_ref, b_ref, y_ref, acc_ref):
    x = x_ref[...]
    xb = x.astype(bf16)
    for c in range(D_FF // FF_CHUNK):
        gate = jnp.dot(xb, win_ref[:, c * FF_CHUNK:(c + 1) * FF_CHUNK], preferred_element_type=f32)
        lin = jnp.dot(xb, win_ref[:, D_FF + c * FF_CHUNK:D_FF + (c + 1) * FF_CHUNK],
                      preferred_element_type=f32)
        act = (jax.nn.silu(gate) * lin).astype(bf16)
        part = jnp.dot(act, wout_ref[c * FF_CHUNK:(c + 1) * FF_CHUNK, :], preferred_element_type=f32)
        if c == 0:
            acc_ref[...] = part
        else:
            acc_ref[...] += part
    y_ref[...] = _layer_norm(ALPHA * x + acc_ref[...], g_ref[...], b_ref[...])


def _ffn_call(x, win, wout, g, b):
    t = x.shape[0]
    tile = lambda: pl.BlockSpec((TOKEN_TILE, D_MODEL), lambda i: (i, 0))
    return pl.pallas_call(
        _ffn_kernel,
        grid=(t // TOKEN_TILE,),
        in_specs=[tile(), _const_spec(win.shape), _const_spec(wout.shape),
                  _const_spec((1, D_MODEL)), _const_spec((1, D_MODEL))],
        out_specs=tile(),
        out_shape=jax.ShapeDtypeStruct((t, D_MODEL), f32),
        scratch_shapes=[pltpu.VMEM((TOKEN_TILE, D_MODEL), f32)],
        compiler_params=_params(48),
        name="ffn",
    )(x, win, wout, g, b)


def _trunk(x, n_img, w):
    t = x.shape[0]
    rows = t // (n_img * GRID_W)
    for i in range(DEPTH):
        j = i // 2
        row = lambda a: a[i].reshape(1, D_MODEL)
        if i % 2 == 0:
            qkv = _qkv_call(x, w["na_w_in"][j])
            attn = _na_call(qkv, w["na_bias"][j], n_img, rows)
            x = _proj_ln_call(x, attn, w["na_w_out"][j], row(w["ln_mix_g"]), row(w["ln_mix_b"]))
        else:
            x = _sg_call(x, w["sg_w_in"][j], w["sg_ln_g"][j].reshape(1, SG_WIDTH),
                         w["sg_ln_b"][j].reshape(1, SG_WIDTH), w["sg_w_sp"][j], w["sg_b_sp"][j],
                         w["sg_w_out"][j], row(w["ln_mix_g"]), row(w["ln_mix_b"]))
        x = _ffn_call(x, w["ffn_w_in"][i], w["ffn_w_out"][i], row(w["ln_ffn_g"]), row(w["ln_ffn_b"]))
    return x


def kernel(x_prompt, x_sample, na_w_in, na_rpb, na_w_out, sg_w_in, sg_ln_g, sg_ln_b, sg_w_s, sg_b_s,
           sg_w_out, ln_mix_g, ln_mix_b, ffn_w_in, ffn_w_out, ln_ffn_g, ln_ffn_b):
    n_sg = sg_w_s.shape[0]
    group_dim = SG_WIDTH // SG_GROUPS
    w = {
        "na_w_in": na_w_in.astype(bf16),
        "na_bias": jnp.stack([_na_bias_table(na_rpb[j]) for j in range(na_rpb.shape[0])]),
        "na_w_out": na_w_out.astype(bf16),
        "sg_w_in": sg_w_in.astype(bf16),
        "sg_ln_g": sg_ln_g, "sg_ln_b": sg_ln_b,
        "sg_w_sp": sg_w_s.astype(bf16).reshape(n_sg, SG_GROUPS // 2, 2, CHUNK, CHUNK)
                   .transpose(0, 1, 3, 2, 4).reshape(n_sg, SG_GROUPS // 2, CHUNK, 2 * CHUNK),
        "sg_b_sp": jnp.repeat(sg_b_s.reshape(n_sg, SG_GROUPS // 2, 2, CHUNK).transpose(0, 1, 3, 2),
                              group_dim, axis=-1),
        "sg_w_out": sg_w_out.astype(bf16),
        "ln_mix_g": ln_mix_g, "ln_mix_b": ln_mix_b,
        "ffn_w_in": ffn_w_in.astype(bf16), "ffn_w_out": ffn_w_out.astype(bf16),
        "ln_ffn_g": ln_ffn_g, "ln_ffn_b": ln_ffn_b,
    }
    b, s, d = x_prompt.shape
    y_prompt = _trunk(x_prompt.reshape(b * s, d), b, w).reshape(b, s, d)
    b, s, d = x_sample.shape
    y_sample = _trunk(x_sample.reshape(b * s, d), b, w).reshape(b, s, d)
    return (y_prompt, y_sample)
```

```python
import functools

import numpy as np
import jax
import jax.numpy as jnp
from jax import lax
from jax.experimental import pallas as pl
from jax.experimental.pallas import tpu as pltpu

D_MODEL = 1024
DEPTH = 4
GRID_W = 64
NA_HEADS = 16
HEAD_DIM = D_MODEL // NA_HEADS
MAX_KH = 8
KW = 16
CHUNK = 128
SG_GROUPS = 16
SG_WIDTH = D_MODEL
D_FF = 2816
ALPHA = (2 * DEPTH) ** 0.25
LN_EPS = 1e-5

LANES = 128
HEAD_PAIRS = D_MODEL // LANES
ROW_BLOCK = 8
KV_ROWS = 2 * ROW_BLOCK
TOKEN_TILE = 512
FF_CHUNK = 256
MASK_BIAS = -1e30
MIB = 1024 * 1024

f32 = jnp.float32
bf16 = jnp.bfloat16


def _layer_norm(y, g, b):
    mu = jnp.mean(y, axis=-1, keepdims=True)
    yc = y - mu
    var = jnp.mean(yc * yc, axis=-1, keepdims=True)
    return yc * lax.rsqrt(var + LN_EPS) * g + b


def _gelu(z):
    return 0.5 * z * (1.0 + lax.erf(z * (0.5 ** 0.5)))


def _const_spec(shape):
    zeros = (0,) * len(shape)
    return pl.BlockSpec(shape, lambda *_: zeros, pipeline_mode=pl.Buffered(1))


def _params(vmem_mib, n_grid=1):
    return pltpu.CompilerParams(
        dimension_semantics=("arbitrary",) * n_grid,
        vmem_limit_bytes=vmem_mib * MIB)


def _qkv_kernel(x_ref, w_ref, o_ref):
    xb = x_ref[...].astype(bf16)
    for c in range(3):
        cols = slice(c * D_MODEL, (c + 1) * D_MODEL)
        acc = jnp.dot(xb, w_ref[:, cols], preferred_element_type=f32)
        if c == 0:
            acc = acc * (HEAD_DIM ** -0.5)
        o_ref[:, cols] = acc.astype(bf16)


def _qkv_call(x, w):
    t = x.shape[0]
    return pl.pallas_call(
        _qkv_kernel,
        grid=(t // TOKEN_TILE,),
        in_specs=[pl.BlockSpec((TOKEN_TILE, D_MODEL), lambda i: (i, 0)),
                  _const_spec((D_MODEL, 3 * D_MODEL))],
        out_specs=pl.BlockSpec((TOKEN_TILE, 3 * D_MODEL), lambda i: (i, 0)),
        out_shape=jax.ShapeDtypeStruct((t, 3 * D_MODEL), bf16),
        compiler_params=_params(40),
        name="na_qkv",
    )(x, w)


def _na_kernel(q_ref, k_ref, v_ref, bias_ref, o_ref, *, rows):
    i = pl.program_id(1)
    wstart = jnp.clip(ROW_BLOCK * i - MAX_KH // 2, 0, rows - KV_ROWS)
    lane = lax.broadcasted_iota(jnp.int32, (GRID_W, LANES), 1)
    first_head = lane < HEAD_DIM
    nkeys = MAX_KH * GRID_W
    for rr in range(ROW_BLOCK):
        r = ROW_BLOCK * i + rr
        rs = jnp.clip(r - MAX_KH // 2, 0, rows - MAX_KH)
        ks = pl.multiple_of((rs - wstart) * GRID_W, GRID_W)
        bs = pl.multiple_of((rs - r + MAX_KH - 1) * GRID_W, GRID_W)
        qrows = slice(rr * GRID_W, (rr + 1) * GRID_W)
        for hp in range(HEAD_PAIRS):
            cols = slice(hp * LANES, (hp + 1) * LANES)
            qp = q_ref[qrows, cols]
            zero = jnp.zeros_like(qp)
            qm = jnp.concatenate([jnp.where(first_head, qp, zero), jnp.where(first_head, zero, qp)], axis=0)
            kp = k_ref[pl.ds(ks, nkeys), cols]
            vp = v_ref[pl.ds(ks, nkeys), cols]
            s = lax.dot_general(kp, qm, (((1,), (1,)), ((), ())), preferred_element_type=f32)
            s = s + bias_ref[hp, pl.ds(bs, nkeys), :]
            m = jnp.max(s, axis=0, keepdims=True)
            p = jnp.exp(s - m)
            inv = 1.0 / jnp.sum(p, axis=0, keepdims=True)
            pn = (p * inv).astype(bf16)
            pv = lax.dot_general(pn, vp, (((0,), (0,)), ((), ())), preferred_element_type=f32)
            o_ref[qrows, cols] = jnp.where(first_head, pv[:GRID_W], pv[GRID_W:]).astype(bf16)


def _na_call(qkv, bias, n_img, rows):
    t = qkv.shape[0]
    nb = rows // ROW_BLOCK
    blk = ROW_BLOCK * GRID_W
    win = KV_ROWS * GRID_W

    def kv_map(col):
        def index_map(b, i):
            wstart = jnp.clip(ROW_BLOCK * i - MAX_KH // 2, 0, rows - KV_ROWS)
            return ((b * rows + wstart) * GRID_W, col * D_MODEL)
        return index_map

    kv_spec = lambda col: pl.BlockSpec((pl.Element(win), pl.Element(D_MODEL)), kv_map(col))
    return pl.pallas_call(
        functools.partial(_na_kernel, rows=rows),
        grid=(n_img, nb),
        in_specs=[pl.BlockSpec((blk, D_MODEL), lambda b, i: (b * nb + i, 0)),
                  kv_spec(1), kv_spec(2),
                  _const_spec(bias.shape)],
        out_specs=pl.BlockSpec((blk, D_MODEL), lambda b, i: (b * nb + i, 0)),
        out_shape=jax.ShapeDtypeStruct((t, D_MODEL), bf16),
        compiler_params=_params(48, 2),
        name="na_attn",
    )(qkv, qkv, qkv, bias)


def _na_bias_table(rpb):
    n_dr = 2 * MAX_KH - 1
    c = np.arange(GRID_W)
    col_start = np.clip(c - KW // 2, 0, GRID_W - KW)
    kc = np.arange(GRID_W)
    in_win = (kc[:, None] >= col_start[None, :]) & (kc[:, None] < col_start[None, :] + KW)
    pad = GRID_W - KW
    padded = jnp.pad(rpb.astype(f32), ((0, 0), (0, 0), (pad, pad)))
    t = jnp.stack([padded[:, :, GRID_W - 1 - cc:2 * GRID_W - 1 - cc] for cc in range(GRID_W)], axis=-1)
    t = jnp.where(in_win[None, None], t, MASK_BIAS)
    t = t.reshape(HEAD_PAIRS, 2, n_dr, GRID_W, GRID_W).transpose(0, 2, 3, 1, 4)
    return t.reshape(HEAD_PAIRS, n_dr * GRID_W, LANES)


def _proj_ln_kernel(x_ref, a_ref, w_ref, g_ref, b_ref, y_ref):
    m = jnp.dot(a_ref[...], w_ref[...], preferred_element_type=f32)
    y_ref[...] = _layer_norm(ALPHA * x_ref[...] + m, g_ref[...], b_ref[...])


def _proj_ln_call(x, a, w, g, b):
    t = x.shape[0]
    tile = lambda: pl.BlockSpec((TOKEN_TILE, D_MODEL), lambda i: (i, 0))
    return pl.pallas_call(
        _proj_ln_kernel,
        grid=(t // TOKEN_TILE,),
        in_specs=[tile(), tile(), _const_spec((D_MODEL, D_MODEL)),
                  _const_spec((1, D_MODEL)), _const_spec((1, D_MODEL))],
        out_specs=tile(),
        out_shape=jax.ShapeDtypeStruct((t, D_MODEL), f32),
        compiler_params=_params(32),
        name="na_out_ln",
    )(x, a, w, g, b)


def _sg_kernel(x_ref, win_ref, lng_ref, lnb_ref, wsp_ref, bsp_ref, wout_ref, g_ref, b_ref,
               y_ref, gated_ref):
    x = x_ref[...]
    xb = x.astype(bf16)
    u = _gelu(jnp.dot(xb, win_ref[:, :SG_WIDTH], preferred_element_type=f32))
    v = _gelu(jnp.dot(xb, win_ref[:, SG_WIDTH:], preferred_element_type=f32))
    vb = _layer_norm(v, lng_ref[...], lnb_ref[...]).astype(bf16)
    lane = lax.broadcasted_iota(jnp.int32, (CHUNK, LANES), 1)
    first_group = lane < SG_WIDTH // SG_GROUPS
    zero = jnp.zeros((CHUNK, LANES), bf16)
    for n in range(TOKEN_TILE // CHUNK):
        toks = slice(n * CHUNK, (n + 1) * CHUNK)
        for j in range(SG_GROUPS // 2):
            cols = slice(j * LANES, (j + 1) * LANES)
            vp = vb[toks, cols]
            rhs = jnp.concatenate([jnp.where(first_group, vp, zero),
                                   jnp.where(first_group, zero, vp)], axis=0)
            sp = jnp.dot(wsp_ref[j], rhs, preferred_element_type=f32) + bsp_ref[j]
            gated_ref[toks, cols] = (u[toks, cols] * sp).astype(bf16)
    m = jnp.dot(gated_ref[...], wout_ref[...], preferred_element_type=f32)
    y_ref[...] = _layer_norm(ALPHA * x + m, g_ref[...], b_ref[...])


def _sg_call(x, win, lng, lnb, wsp, bsp, wout, g, b):
    t = x.shape[0]
    tile = lambda: pl.BlockSpec((TOKEN_TILE, D_MODEL), lambda i: (i, 0))
    return pl.pallas_call(
        _sg_kernel,
        grid=(t // TOKEN_TILE,),
        in_specs=[tile(), _const_spec(win.shape), _const_spec((1, SG_WIDTH)), _const_spec((1, SG_WIDTH)),
                  _const_spec(wsp.shape), _const_spec(bsp.shape), _const_spec(wout.shape),
                  _const_spec((1, D_MODEL)), _const_spec((1, D_MODEL))],
        out_specs=tile(),
        out_shape=jax.ShapeDtypeStruct((t, D_MODEL), f32),
        scratch_shapes=[pltpu.VMEM((TOKEN_TILE, SG_WIDTH), bf16)],
        compiler_params=_params(48),
        name="sg_layer",
    )(x, win, lng, lnb, wsp, bsp, wout, g, b)


def _ffn_kernel(x_ref, win_ref, wout_ref, g_ref, b_ref, y_ref, acc_ref):
    x = x_ref[...]
    xb = x.astype(bf16)
    for c in range(D_FF // FF_CHUNK):
        gate = jnp.dot(xb, win_ref[:, c * FF_CHUNK:(c + 1) * FF_CHUNK], preferred_element_type=f32)
        lin = jnp.dot(xb, win_ref[:, D_FF + c * FF_CHUNK:D_FF + (c + 1) * FF_CHUNK],
                      preferred_element_type=f32)
        act = (jax.nn.silu(gate) * lin).astype(bf16)
        part = jnp.dot(act, wout_ref[c * FF_CHUNK:(c + 1) * FF_CHUNK, :], preferred_element_type=f32)
        if c == 0:
            acc_ref[...] = part
        else:
            acc_ref[...] += part
    y_ref[...] = _layer_norm(ALPHA * x + acc_ref[...], g_ref[...], b_ref[...])


def _ffn_call(x, win, wout, g, b):
    t = x.shape[0]
    tile = lambda: pl.BlockSpec((TOKEN_TILE, D_MODEL), lambda i: (i, 0))
    return pl.pallas_call(
        _ffn_kernel,
        grid=(t // TOKEN_TILE,),
        in_specs=[tile(), _const_spec(win.shape), _const_spec(wout.shape),
                  _const_spec((1, D_MODEL)), _const_spec((1, D_MODEL))],
        out_specs=tile(),
        out_shape=jax.ShapeDtypeStruct((t, D_MODEL), f32),
        scratch_shapes=[pltpu.VMEM((TOKEN_TILE, D_MODEL), f32)],
        compiler_params=_params(48),
        name="ffn",
    )(x, win, wout, g, b)


def _trunk(x, n_img, w):
    t = x.shape[0]
    rows = t // (n_img * GRID_W)
    for i in range(DEPTH):
        j = i // 2
        row = lambda a: a[i].reshape(1, D_MODEL)
        if i % 2 == 0:
            qkv = _qkv_call(x, w["na_w_in"][j])
            attn = _na_call(qkv, w["na_bias"][j], n_img, rows)
            x = _proj_ln_call(x, attn, w["na_w_out"][j], row(w["ln_mix_g"]), row(w["ln_mix_b"]))
        else:
            x = _sg_call(x, w["sg_w_in"][j], w["sg_ln_g"][j].reshape(1, SG_WIDTH),
                         w["sg_ln_b"][j].reshape(1, SG_WIDTH), w["sg_w_sp"][j], w["sg_b_sp"][j],
                         w["sg_w_out"][j], row(w["ln_mix_g"]), row(w["ln_mix_b"]))
        x = _ffn_call(x, w["ffn_w_in"][i], w["ffn_w_out"][i], row(w["ln_ffn_g"]), row(w["ln_ffn_b"]))
    return x


def kernel(x_prompt, x_sample, na_w_in, na_rpb, na_w_out, sg_w_in, sg_ln_g, sg_ln_b, sg_w_s, sg_b_s,
           sg_w_out, ln_mix_g, ln_mix_b, ffn_w_in, ffn_w_out, ln_ffn_g, ln_ffn_b):
    n_sg = sg_w_s.shape[0]
    group_dim = SG_WIDTH // SG_GROUPS
    w = {
        "na_w_in": na_w_in.astype(bf16),
        "na_bias": jnp.stack([_na_bias_table(na_rpb[j]) for j in range(na_rpb.shape[0])]),
        "na_w_out": na_w_out.astype(bf16),
        "sg_w_in": sg_w_in.astype(bf16),
        "sg_ln_g": sg_ln_g, "sg_ln_b": sg_ln_b,
        "sg_w_sp": sg_w_s.astype(bf16).reshape(n_sg, SG_GROUPS // 2, 2, CHUNK, CHUNK)
                   .transpose(0, 1, 3, 2, 4).reshape(n_sg, SG_GROUPS // 2, CHUNK, 2 * CHUNK),
        "sg_b_sp": jnp.repeat(sg_b_s.reshape(n_sg, SG_GROUPS // 2, 2, CHUNK).transpose(0, 1, 3, 2),
                              group_dim, axis=-1),
        "sg_w_out": sg_w_out.astype(bf16),
        "ln_mix_g": ln_mix_g, "ln_mix_b": ln_mix_b,
        "ffn_w_in": ffn_w_in.astype(bf16), "ffn_w_out": ffn_w_out.astype(bf16),
        "ln_ffn_g": ln_ffn_g, "ln_ffn_b": ln_ffn_b,
    }
    b, s, d = x_prompt.shape
    y_prompt = _trunk(x_prompt.reshape(b * s, d), b, w).reshape(b, s, d)
    b, s, d = x_sample.shape
    y_sample = _trunk(x_sample.reshape(b * s, d), b, w).reshape(b, s, d)
    return (y_prompt, y_sample)
```

```python
import functools

import numpy as np
import jax
import jax.numpy as jnp
from jax import lax
from jax.experimental import pallas as pl
from jax.experimental.pallas import tpu as pltpu

D_MODEL = 1024
DEPTH = 4
GRID_W = 64
NA_HEADS = 16
HEAD_DIM = D_MODEL // NA_HEADS
MAX_KH = 8
KW = 16
CHUNK = 128
SG_GROUPS = 16
SG_WIDTH = D_MODEL
D_FF = 2816
ALPHA = (2 * DEPTH) ** 0.25
LN_EPS = 1e-5

LANES = 128
HEAD_PAIRS = D_MODEL // LANES
ROW_BLOCK = 8
KV_ROWS = 2 * ROW_BLOCK
TOKEN_TILE = 512
SG_TILE, SG_SUB = 512, 256
FFN_TILE, FFN_SUB = 1024, 512
FF_CHUNK = 256
MASK_BIAS = -1e30
MIB = 1024 * 1024

f32 = jnp.float32
bf16 = jnp.bfloat16


def _layer_norm(y, g, b):
    mu = jnp.mean(y, axis=-1, keepdims=True)
    yc = y - mu
    var = jnp.mean(yc * yc, axis=-1, keepdims=True)
    return yc * lax.rsqrt(var + LN_EPS) * g + b


def _gelu(z):
    return 0.5 * z * (1.0 + lax.erf(z * (0.5 ** 0.5)))


def _const_spec(shape):
    zeros = (0,) * len(shape)
    return pl.BlockSpec(shape, lambda *_: zeros, pipeline_mode=pl.Buffered(1))


def _params(vmem_mib, n_grid=1):
    return pltpu.CompilerParams(
        dimension_semantics=("arbitrary",) * n_grid,
        vmem_limit_bytes=vmem_mib * MIB)


def _qkv_kernel(x_ref, w_ref, o_ref):
    xb = x_ref[...].astype(bf16)
    for c in range(3):
        cols = slice(c * D_MODEL, (c + 1) * D_MODEL)
        acc = jnp.dot(xb, w_ref[:, cols], preferred_element_type=f32)
        if c == 0:
            acc = acc * (HEAD_DIM ** -0.5)
        o_ref[:, cols] = acc.astype(bf16)


def _qkv_call(x, w):
    t = x.shape[0]
    return pl.pallas_call(
        _qkv_kernel,
        grid=(t // TOKEN_TILE,),
        in_specs=[pl.BlockSpec((TOKEN_TILE, D_MODEL), lambda i: (i, 0)),
                  _const_spec((D_MODEL, 3 * D_MODEL))],
        out_specs=pl.BlockSpec((TOKEN_TILE, 3 * D_MODEL), lambda i: (i, 0)),
        out_shape=jax.ShapeDtypeStruct((t, 3 * D_MODEL), bf16),
        compiler_params=_params(40),
        name="na_qkv",
    )(x, w)


def _na_kernel(q_ref, k_ref, v_ref, bias_ref, o_ref, *, rows):
    i = pl.program_id(1)
    wstart = jnp.clip(ROW_BLOCK * i - MAX_KH // 2, 0, rows - KV_ROWS)
    lane = lax.broadcasted_iota(jnp.int32, (GRID_W, LANES), 1)
    first_head = lane < HEAD_DIM
    nkeys = MAX_KH * GRID_W
    for rr in range(ROW_BLOCK):
        r = ROW_BLOCK * i + rr
        rs = jnp.clip(r - MAX_KH // 2, 0, rows - MAX_KH)
        ks = pl.multiple_of((rs - wstart) * GRID_W, GRID_W)
        bs = pl.multiple_of((rs - r + MAX_KH - 1) * GRID_W, GRID_W)
        qrows = slice(rr * GRID_W, (rr + 1) * GRID_W)
        for hp in range(HEAD_PAIRS):
            cols = slice(hp * LANES, (hp + 1) * LANES)
            qp = q_ref[qrows, cols]
            zero = jnp.zeros_like(qp)
            qm = jnp.concatenate([jnp.where(first_head, qp, zero), jnp.where(first_head, zero, qp)], axis=0)
            kp = k_ref[pl.ds(ks, nkeys), cols]
            vp = v_ref[pl.ds(ks, nkeys), cols]
            s = lax.dot_general(kp, qm, (((1,), (1,)), ((), ())), preferred_element_type=f32)
            s = s + bias_ref[hp, pl.ds(bs, nkeys), :]
            m = jnp.max(s, axis=0, keepdims=True)
            p = jnp.exp(s - m)
            inv = 1.0 / jnp.sum(p, axis=0, keepdims=True)
            pn = (p * inv).astype(bf16)
            pv = lax.dot_general(pn, vp, (((0,), (0,)), ((), ())), preferred_element_type=f32)
            o_ref[qrows, cols] = jnp.where(first_head, pv[:GRID_W], pv[GRID_W:]).astype(bf16)


def _na_call(qkv, bias, n_img, rows):
    t = qkv.shape[0]
    nb = rows // ROW_BLOCK
    blk = ROW_BLOCK * GRID_W
    win = KV_ROWS * GRID_W

    def kv_map(col):
        def index_map(b, i):
            wstart = jnp.clip(ROW_BLOCK * i - MAX_KH // 2, 0, rows - KV_ROWS)
            return ((b * rows + wstart) * GRID_W, col * D_MODEL)
        return index_map

    kv_spec = lambda col: pl.BlockSpec((pl.Element(win), pl.Element(D_MODEL)), kv_map(col))
    return pl.pallas_call(
        functools.partial(_na_kernel, rows=rows),
        grid=(n_img, nb),
        in_specs=[pl.BlockSpec((blk, D_MODEL), lambda b, i: (b * nb + i, 0)),
                  kv_spec(1), kv_spec(2),
                  _const_spec(bias.shape)],
        out_specs=pl.BlockSpec((blk, D_MODEL), lambda b, i: (b * nb + i, 0)),
        out_shape=jax.ShapeDtypeStruct((t, D_MODEL), bf16),
        compiler_params=_params(48, 2),
        name="na_attn",
    )(qkv, qkv, qkv, bias)


def _na_bias_table(rpb):
    n_dr = 2 * MAX_KH - 1
    c = np.arange(GRID_W)
    col_start = np.clip(c - KW // 2, 0, GRID_W - KW)
    kc = np.arange(GRID_W)
    in_win = (kc[:, None] >= col_start[None, :]) & (kc[:, None] < col_start[None, :] + KW)
    e = np.arange(2 * KW - 1)
    select = (e[:, None, None] == (kc[:, None] - c[None, :] + KW - 1)[None]) & in_win[None]
    t = jnp.einsum("hde,ekc->hdkc", rpb.astype(f32), jnp.asarray(select, f32),
                   precision=lax.Precision.HIGHEST)
    t = jnp.where(in_win[None, None], t, MASK_BIAS)
    t = t.reshape(HEAD_PAIRS, 2, n_dr, GRID_W, GRID_W).transpose(0, 2, 3, 1, 4)
    return t.reshape(HEAD_PAIRS, n_dr * GRID_W, LANES)


def _sg_kernel(x_ref, win_ref, lng_ref, lnb_ref, wsp_ref, bsp_ref, wout_ref, g_ref, b_ref,
               y_ref, gated_ref):
    lane = lax.broadcasted_iota(jnp.int32, (CHUNK, LANES), 1)
    first_group = lane < SG_WIDTH // SG_GROUPS
    zero = jnp.zeros((CHUNK, LANES), bf16)
    for s in range(SG_TILE // SG_SUB):
        rows = slice(s * SG_SUB, (s + 1) * SG_SUB)
        x = x_ref[rows, :]
        xb = x.astype(bf16)
        u = _gelu(jnp.dot(xb, win_ref[:, :SG_WIDTH], preferred_element_type=f32))
        v = _gelu(jnp.dot(xb, win_ref[:, SG_WIDTH:], preferred_element_type=f32))
        vb = _layer_norm(v, lng_ref[...], lnb_ref[...]).astype(bf16)
        for n in range(SG_SUB // CHUNK):
            toks = slice(n * CHUNK, (n + 1) * CHUNK)
            for j in range(SG_GROUPS // 2):
                cols = slice(j * LANES, (j + 1) * LANES)
                vp = vb[toks, cols]
                rhs = jnp.concatenate([jnp.where(first_group, vp, zero),
                                       jnp.where(first_group, zero, vp)], axis=0)
                sp = jnp.dot(wsp_ref[j], rhs, preferred_element_type=f32) + bsp_ref[j]
                gated_ref[s * SG_SUB + n * CHUNK:s * SG_SUB + (n + 1) * CHUNK, cols] = (
                    u[toks, cols] * sp).astype(bf16)
        m = jnp.dot(gated_ref[rows, :], wout_ref[...], preferred_element_type=f32)
        y_ref[rows, :] = _layer_norm(ALPHA * x + m, g_ref[...], b_ref[...])


def _sg_call(x, win, lng, lnb, wsp, bsp, wout, g, b):
    t = x.shape[0]
    tile = lambda: pl.BlockSpec((SG_TILE, D_MODEL), lambda i: (i, 0))
    return pl.pallas_call(
        _sg_kernel,
        grid=(t // SG_TILE,),
        in_specs=[tile(), _const_spec(win.shape), _const_spec((1, SG_WIDTH)), _const_spec((1, SG_WIDTH)),
                  _const_spec(wsp.shape), _const_spec(bsp.shape), _const_spec(wout.shape),
                  _const_spec((1, D_MODEL)), _const_spec((1, D_MODEL))],
        out_specs=tile(),
        out_shape=jax.ShapeDtypeStruct((t, D_MODEL), f32),
        scratch_shapes=[pltpu.VMEM((SG_TILE, SG_WIDTH), bf16)],
        compiler_params=_params(48),
        name="sg_layer",
    )(x, win, lng, lnb, wsp, bsp, wout, g, b)


def _ffn_kernel(*refs, with_proj):
    if with_proj:
        x_ref, a_ref, wp_ref, gm_ref, bm_ref, win_ref, wout_ref, g_ref, b_ref, y_ref, acc_ref = refs
    else:
        x_ref, win_ref, wout_ref, g_ref, b_ref, y_ref, acc_ref = refs
    for s in range(FFN_TILE // FFN_SUB):
        rows = slice(s * FFN_SUB, (s + 1) * FFN_SUB)
        if with_proj:
            m = jnp.dot(a_ref[rows, :], wp_ref[...], preferred_element_type=f32)
            y_ref[rows, :] = _layer_norm(ALPHA * x_ref[rows, :] + m, gm_ref[...], bm_ref[...])
            src_ref = y_ref
        else:
            src_ref = x_ref
        xb = src_ref[rows, :].astype(bf16)
        for c in range(D_FF // FF_CHUNK):
            gate = jnp.dot(xb, win_ref[:, c * FF_CHUNK:(c + 1) * FF_CHUNK], preferred_element_type=f32)
            lin = jnp.dot(xb, win_ref[:, D_FF + c * FF_CHUNK:D_FF + (c + 1) * FF_CHUNK],
                          preferred_element_type=f32)
            act = (jax.nn.silu(gate) * lin).astype(bf16)
            part = jnp.dot(act, wout_ref[c * FF_CHUNK:(c + 1) * FF_CHUNK, :], preferred_element_type=f32)
            if c == 0:
                acc_ref[rows, :] = part
            else:
                acc_ref[rows, :] += part
        y_ref[rows, :] = _layer_norm(ALPHA * src_ref[rows, :] + acc_ref[rows, :], g_ref[...], b_ref[...])


def _ffn_call(x, win, wout, g, b, proj=None):
    t = x.shape[0]
    tile = lambda: pl.BlockSpec((FFN_TILE, D_MODEL), lambda i: (i, 0))
    vec = lambda: _const_spec((1, D_MODEL))
    in_specs, args = [tile()], [x]
    if proj is not None:
        in_specs += [tile(), _const_spec((D_MODEL, D_MODEL)), vec(), vec()]
        args += list(proj)
    in_specs += [_const_spec(win.shape), _const_spec(wout.shape), vec(), vec()]
    args += [win, wout, g, b]
    return pl.pallas_call(
        functools.partial(_ffn_kernel, with_proj=proj is not None),
        grid=(t // FFN_TILE,),
        in_specs=in_specs,
        out_specs=tile(),
        out_shape=jax.ShapeDtypeStruct((t, D_MODEL), f32),
        scratch_shapes=[pltpu.VMEM((FFN_TILE, D_MODEL), f32)],
        compiler_params=_params(56),
        name="ffn_proj" if proj is not None else "ffn",
    )(*args)


def _trunk(x, n_img, w):
    t = x.shape[0]
    rows = t // (n_img * GRID_W)
    for i in range(DEPTH):
        j = i // 2
        row = lambda a: a[i].reshape(1, D_MODEL)
        if i % 2 == 0:
            qkv = _qkv_call(x, w["na_w_in"][j])
            attn = _na_call(qkv, w["na_bias"][j], n_img, rows)
            proj = (attn, w["na_w_out"][j], row(w["ln_mix_g"]), row(w["ln_mix_b"]))
        else:
            x = _sg_call(x, w["sg_w_in"][j], w["sg_ln_g"][j].reshape(1, SG_WIDTH),
                         w["sg_ln_b"][j].reshape(1, SG_WIDTH), w["sg_w_sp"][j], w["sg_b_sp"][j],
                         w["sg_w_out"][j], row(w["ln_mix_g"]), row(w["ln_mix_b"]))
            proj = None
        x = _ffn_call(x, w["ffn_w_in"][i], w["ffn_w_out"][i], row(w["ln_ffn_g"]), row(w["ln_ffn_b"]), proj)
    return x


def kernel(x_prompt, x_sample, na_w_in, na_rpb, na_w_out, sg_w_in, sg_ln_g, sg_ln_b, sg_w_s, sg_b_s,
           sg_w_out, ln_mix_g, ln_mix_b, ffn_w_in, ffn_w_out, ln_ffn_g, ln_ffn_b):
    n_sg = sg_w_s.shape[0]
    group_dim = SG_WIDTH // SG_GROUPS
    w = {
        "na_w_in": na_w_in.astype(bf16),
        "na_bias": jnp.stack([_na_bias_table(na_rpb[j]) for j in range(na_rpb.shape[0])]),
        "na_w_out": na_w_out.astype(bf16),
        "sg_w_in": sg_w_in.astype(bf16),
        "sg_ln_g": sg_ln_g, "sg_ln_b": sg_ln_b,
        "sg_w_sp": sg_w_s.astype(bf16).reshape(n_sg, SG_GROUPS // 2, 2, CHUNK, CHUNK)
                   .transpose(0, 1, 3, 2, 4).reshape(n_sg, SG_GROUPS // 2, CHUNK, 2 * CHUNK),
        "sg_b_sp": jnp.repeat(sg_b_s.reshape(n_sg, SG_GROUPS // 2, 2, CHUNK).transpose(0, 1, 3, 2),
                              group_dim, axis=-1),
        "sg_w_out": sg_w_out.astype(bf16),
        "ln_mix_g": ln_mix_g, "ln_mix_b": ln_mix_b,
        "ffn_w_in": ffn_w_in.astype(bf16), "ffn_w_out": ffn_w_out.astype(bf16),
        "ln_ffn_g": ln_ffn_g, "ln_ffn_b": ln_ffn_b,
    }
    b, s, d = x_prompt.shape
    y_prompt = _trunk(x_prompt.reshape(b * s, d), b, w).reshape(b, s, d)
    b, s, d = x_sample.shape
    y_sample = _trunk(x_sample.reshape(b * s, d), b, w).reshape(b, s, d)
    return (y_prompt, y_sample)
```

```python
import functools

import numpy as np
import jax
import jax.numpy as jnp
from jax import lax
from jax.experimental import pallas as pl
from jax.experimental.pallas import tpu as pltpu

D_MODEL = 1024
DEPTH = 4
GRID_W = 64
NA_HEADS = 16
HEAD_DIM = D_MODEL // NA_HEADS
MAX_KH = 8
KW = 16
CHUNK = 128
SG_GROUPS = 16
SG_WIDTH = D_MODEL
D_FF = 2816
ALPHA = (2 * DEPTH) ** 0.25
LN_EPS = 1e-5

LANES = 128
MXU_WIDTH = 256
GROUP_HEADS = MXU_WIDTH // HEAD_DIM
HEAD_GROUPS = D_MODEL // MXU_WIDTH
ROW_BLOCK = 8
KV_ROWS = 2 * ROW_BLOCK
TOKEN_TILE = 512
SG_TILE, SG_SUB = 512, 256
FFN_TILE, FFN_SUB = 1024, 512
FF_CHUNK = 256
MASK_BIAS = -1e30
LOG2_E = 1.4426950408889634
MIB = 1024 * 1024

f32 = jnp.float32
bf16 = jnp.bfloat16


def _layer_norm(y, g, b):
    mu = jnp.mean(y, axis=-1, keepdims=True)
    yc = y - mu
    var = jnp.mean(yc * yc, axis=-1, keepdims=True)
    return yc * lax.rsqrt(var + LN_EPS) * g + b


def _gelu(z):
    return 0.5 * z * (1.0 + lax.erf(z * (0.5 ** 0.5)))


def _const_spec(shape):
    zeros = (0,) * len(shape)
    return pl.BlockSpec(shape, lambda *_: zeros, pipeline_mode=pl.Buffered(1))


def _params(vmem_mib, n_grid=1):
    return pltpu.CompilerParams(
        dimension_semantics=("arbitrary",) * n_grid,
        vmem_limit_bytes=vmem_mib * MIB)


def _qkv_kernel(x_ref, w_ref, o_ref):
    xb = x_ref[...].astype(bf16)
    for c in range(3):
        cols = slice(c * D_MODEL, (c + 1) * D_MODEL)
        acc = jnp.dot(xb, w_ref[:, cols], preferred_element_type=f32)
        if c == 0:
            acc = acc * (HEAD_DIM ** -0.5 * LOG2_E)
        o_ref[:, cols] = acc.astype(bf16)


def _qkv_call(x, w):
    t = x.shape[0]
    return pl.pallas_call(
        _qkv_kernel,
        grid=(t // TOKEN_TILE,),
        in_specs=[pl.BlockSpec((TOKEN_TILE, D_MODEL), lambda i: (i, 0)),
                  _const_spec((D_MODEL, 3 * D_MODEL))],
        out_specs=pl.BlockSpec((TOKEN_TILE, 3 * D_MODEL), lambda i: (i, 0)),
        out_shape=jax.ShapeDtypeStruct((t, 3 * D_MODEL), bf16),
        compiler_params=_params(40),
        name="na_qkv",
    )(x, w)


def _na_kernel(q_ref, k_ref, v_ref, bias_ref, o_ref, *, rows):
    i = pl.program_id(1)
    wstart = jnp.clip(ROW_BLOCK * i - MAX_KH // 2, 0, rows - KV_ROWS)
    lane_head = lax.broadcasted_iota(jnp.int32, (GRID_W, MXU_WIDTH), 1) // HEAD_DIM
    nkeys = MAX_KH * GRID_W
    for rr in range(ROW_BLOCK):
        r = ROW_BLOCK * i + rr
        rs = jnp.clip(r - MAX_KH // 2, 0, rows - MAX_KH)
        ks = pl.multiple_of((rs - wstart) * GRID_W, GRID_W)
        bs = pl.multiple_of((rs - r + MAX_KH - 1) * GRID_W, GRID_W)
        qrows = slice(rr * GRID_W, (rr + 1) * GRID_W)
        for hg in range(HEAD_GROUPS):
            cols = slice(hg * MXU_WIDTH, (hg + 1) * MXU_WIDTH)
            qg = q_ref[qrows, cols]
            zero = jnp.zeros_like(qg)
            qm = jnp.concatenate([jnp.where(lane_head == h, qg, zero) for h in range(GROUP_HEADS)], axis=0)
            kg = k_ref[pl.ds(ks, nkeys), cols]
            vg = v_ref[pl.ds(ks, nkeys), cols]
            s = lax.dot_general(kg, qm, (((1,), (1,)), ((), ())), preferred_element_type=f32)
            s = s + bias_ref[hg, pl.ds(bs, nkeys), :]
            m = jnp.max(s, axis=0, keepdims=True)
            p = jnp.exp2(s - m)
            inv = 1.0 / jnp.sum(p, axis=0, keepdims=True)
            pn = (p * inv).astype(bf16)
            pv = lax.dot_general(pn, vg, (((0,), (0,)), ((), ())), preferred_element_type=f32)
            out = pv[:GRID_W]
            for h in range(1, GROUP_HEADS):
                out = jnp.where(lane_head == h, pv[h * GRID_W:(h + 1) * GRID_W], out)
            o_ref[qrows, cols] = out.astype(bf16)


def _na_call(qkv, bias, n_img, rows):
    t = qkv.shape[0]
    nb = rows // ROW_BLOCK
    blk = ROW_BLOCK * GRID_W
    win = KV_ROWS * GRID_W

    def kv_map(col):
        def index_map(b, i):
            wstart = jnp.clip(ROW_BLOCK * i - MAX_KH // 2, 0, rows - KV_ROWS)
            return ((b * rows + wstart) * GRID_W, col * D_MODEL)
        return index_map

    kv_spec = lambda col: pl.BlockSpec((pl.Element(win), pl.Element(D_MODEL)), kv_map(col))
    return pl.pallas_call(
        functools.partial(_na_kernel, rows=rows),
        grid=(n_img, nb),
        in_specs=[pl.BlockSpec((blk, D_MODEL), lambda b, i: (b * nb + i, 0)),
                  kv_spec(1), kv_spec(2),
                  _const_spec(bias.shape)],
        out_specs=pl.BlockSpec((blk, D_MODEL), lambda b, i: (b * nb + i, 0)),
        out_shape=jax.ShapeDtypeStruct((t, D_MODEL), bf16),
        compiler_params=_params(48, 2),
        name="na_attn",
    )(qkv, qkv, qkv, bias)


def _na_bias_table(rpb):
    n_dr = 2 * MAX_KH - 1
    c = np.arange(GRID_W)
    col_start = np.clip(c - KW // 2, 0, GRID_W - KW)
    kc = np.arange(GRID_W)
    in_win = (kc[:, None] >= col_start[None, :]) & (kc[:, None] < col_start[None, :] + KW)
    e = np.arange(2 * KW - 1)
    select = (e[:, None, None] == (kc[:, None] - c[None, :] + KW - 1)[None]) & in_win[None]
    t = jnp.einsum("hde,ekc->hdkc", rpb.astype(f32), jnp.asarray(select, f32),
                   precision=lax.Precision.HIGHEST)
    t = jnp.where(in_win[None, None], t * LOG2_E, MASK_BIAS)
    t = t.reshape(HEAD_GROUPS, GROUP_HEADS, n_dr, GRID_W, GRID_W).transpose(0, 2, 3, 1, 4)
    return t.reshape(HEAD_GROUPS, n_dr * GRID_W, MXU_WIDTH)


def _sg_kernel(x_ref, win_ref, lng_ref, lnb_ref, wsp_ref, bsp_ref, wout_ref, g_ref, b_ref,
               y_ref, gated_ref):
    lane = lax.broadcasted_iota(jnp.int32, (CHUNK, LANES), 1)
    first_group = lane < SG_WIDTH // SG_GROUPS
    zero = jnp.zeros((CHUNK, LANES), bf16)
    subs = [slice(s * SG_SUB, (s + 1) * SG_SUB) for s in range(SG_TILE // SG_SUB)]
    raw = []
    for rows in subs:
        xb = x_ref[rows, :].astype(bf16)
        raw.append((jnp.dot(xb, win_ref[:, :SG_WIDTH], preferred_element_type=f32),
                    jnp.dot(xb, win_ref[:, SG_WIDTH:], preferred_element_type=f32)))
    act = [(_gelu(u), _layer_norm(_gelu(v), lng_ref[...], lnb_ref[...]).astype(bf16)) for u, v in raw]
    mixed = []
    for rows, (u, vb) in zip(subs, act):
        for n in range(SG_SUB // CHUNK):
            toks = slice(n * CHUNK, (n + 1) * CHUNK)
            for j in range(SG_GROUPS // 2):
                cols = slice(j * LANES, (j + 1) * LANES)
                vp = vb[toks, cols]
                rhs = jnp.concatenate([jnp.where(first_group, vp, zero),
                                       jnp.where(first_group, zero, vp)], axis=0)
                sp = jnp.dot(wsp_ref[j], rhs, preferred_element_type=f32) + bsp_ref[j]
                gated_ref[rows.start + n * CHUNK:rows.start + (n + 1) * CHUNK, cols] = (
                    u[toks, cols] * sp).astype(bf16)
        mixed.append(jnp.dot(gated_ref[rows, :], wout_ref[...], preferred_element_type=f32))
    for rows, m in zip(subs, mixed):
        y_ref[rows, :] = _layer_norm(ALPHA * x_ref[rows, :] + m, g_ref[...], b_ref[...])


def _sg_call(x, win, lng, lnb, wsp, bsp, wout, g, b):
    t = x.shape[0]
    tile = lambda: pl.BlockSpec((SG_TILE, D_MODEL), lambda i: (i, 0))
    return pl.pallas_call(
        _sg_kernel,
        grid=(t // SG_TILE,),
        in_specs=[tile(), _const_spec(win.shape), _const_spec((1, SG_WIDTH)), _const_spec((1, SG_WIDTH)),
                  _const_spec(wsp.shape), _const_spec(bsp.shape), _const_spec(wout.shape),
                  _const_spec((1, D_MODEL)), _const_spec((1, D_MODEL))],
        out_specs=tile(),
        out_shape=jax.ShapeDtypeStruct((t, D_MODEL), f32),
        scratch_shapes=[pltpu.VMEM((SG_TILE, SG_WIDTH), bf16)],
        compiler_params=_params(48),
        name="sg_layer",
    )(x, win, lng, lnb, wsp, bsp, wout, g, b)


def _ffn_kernel(*refs, with_proj):
    if with_proj:
        x_ref, a_ref, wp_ref, gm_ref, bm_ref, win_ref, wout_ref, g_ref, b_ref, y_ref, acc_ref, xb_ref = refs
        src_ref = y_ref
    else:
        x_ref, win_ref, wout_ref, g_ref, b_ref, y_ref, acc_ref, xb_ref = refs
        src_ref = x_ref
    subs = [slice(s * FFN_SUB, (s + 1) * FFN_SUB) for s in range(FFN_TILE // FFN_SUB)]
    n_chunks = D_FF // FF_CHUNK

    def prologue(rows):
        if with_proj:
            m = jnp.dot(a_ref[rows, :], wp_ref[...], preferred_element_type=f32)
            y_ref[rows, :] = _layer_norm(ALPHA * x_ref[rows, :] + m, gm_ref[...], bm_ref[...])
        xb_ref[rows, :] = src_ref[rows, :].astype(bf16)

    def chunk(rows, c):
        xb = xb_ref[rows, :]
        gate = jnp.dot(xb, win_ref[:, c * FF_CHUNK:(c + 1) * FF_CHUNK], preferred_element_type=f32)
        lin = jnp.dot(xb, win_ref[:, D_FF + c * FF_CHUNK:D_FF + (c + 1) * FF_CHUNK],
                      preferred_element_type=f32)
        act = (jax.nn.silu(gate) * lin).astype(bf16)
        part = jnp.dot(act, wout_ref[c * FF_CHUNK:(c + 1) * FF_CHUNK, :], preferred_element_type=f32)
        if c == 0:
            acc_ref[rows, :] = part
        else:
            acc_ref[rows, :] += part

    def epilogue(rows):
        y_ref[rows, :] = _layer_norm(ALPHA * src_ref[rows, :] + acc_ref[rows, :], g_ref[...], b_ref[...])

    prologue(subs[0])
    for s, rows in enumerate(subs):
        for c in range(n_chunks):
            chunk(rows, c)
            if c == n_chunks // 2 and s + 1 < len(subs):
                prologue(subs[s + 1])
        epilogue(rows)


def _ffn_call(x, win, wout, g, b, proj=None):
    t = x.shape[0]
    tile = lambda: pl.BlockSpec((FFN_TILE, D_MODEL), lambda i: (i, 0))
    vec = lambda: _const_spec((1, D_MODEL))
    in_specs, args = [tile()], [x]
    if proj is not None:
        in_specs += [tile(), _const_spec((D_MODEL, D_MODEL)), vec(), vec()]
        args += list(proj)
    in_specs += [_const_spec(win.shape), _const_spec(wout.shape), vec(), vec()]
    args += [win, wout, g, b]
    return pl.pallas_call(
        functools.partial(_ffn_kernel, with_proj=proj is not None),
        grid=(t // FFN_TILE,),
        in_specs=in_specs,
        out_specs=tile(),
        out_shape=jax.ShapeDtypeStruct((t, D_MODEL), f32),
        scratch_shapes=[pltpu.VMEM((FFN_TILE, D_MODEL), f32), pltpu.VMEM((FFN_TILE, D_MODEL), bf16)],
        compiler_params=_params(56),
        name="ffn_proj" if proj is not None else "ffn",
    )(*args)


def _trunk(x, n_img, w):
    t = x.shape[0]
    rows = t // (n_img * GRID_W)
    for i in range(DEPTH):
        j = i // 2
        row = lambda a: a[i].reshape(1, D_MODEL)
        if i % 2 == 0:
            qkv = _qkv_call(x, w["na_w_in"][j])
            attn = _na_call(qkv, w["na_bias"][j], n_img, rows)
            proj = (attn, w["na_w_out"][j], row(w["ln_mix_g"]), row(w["ln_mix_b"]))
        else:
            x = _sg_call(x, w["sg_w_in"][j], w["sg_ln_g"][j].reshape(1, SG_WIDTH),
                         w["sg_ln_b"][j].reshape(1, SG_WIDTH), w["sg_w_sp"][j], w["sg_b_sp"][j],
                         w["sg_w_out"][j], row(w["ln_mix_g"]), row(w["ln_mix_b"]))
            proj = None
        x = _ffn_call(x, w["ffn_w_in"][i], w["ffn_w_out"][i], row(w["ln_ffn_g"]), row(w["ln_ffn_b"]), proj)
    return x


def kernel(x_prompt, x_sample, na_w_in, na_rpb, na_w_out, sg_w_in, sg_ln_g, sg_ln_b, sg_w_s, sg_b_s,
           sg_w_out, ln_mix_g, ln_mix_b, ffn_w_in, ffn_w_out, ln_ffn_g, ln_ffn_b):
    n_sg = sg_w_s.shape[0]
    group_dim = SG_WIDTH // SG_GROUPS
    w = {
        "na_w_in": na_w_in.astype(bf16),
        "na_bias": jnp.stack([_na_bias_table(na_rpb[j]) for j in range(na_rpb.shape[0])]),
        "na_w_out": na_w_out.astype(bf16),
        "sg_w_in": sg_w_in.astype(bf16),
        "sg_ln_g": sg_ln_g, "sg_ln_b": sg_ln_b,
        "sg_w_sp": sg_w_s.astype(bf16).reshape(n_sg, SG_GROUPS // 2, 2, CHUNK, CHUNK)
                   .transpose(0, 1, 3, 2, 4).reshape(n_sg, SG_GROUPS // 2, CHUNK, 2 * CHUNK),
        "sg_b_sp": jnp.repeat(sg_b_s.reshape(n_sg, SG_GROUPS // 2, 2, CHUNK).transpose(0, 1, 3, 2),
                              group_dim, axis=-1),
        "sg_w_out": sg_w_out.astype(bf16),
        "ln_mix_g": ln_mix_g, "ln_mix_b": ln_mix_b,
        "ffn_w_in": ffn_w_in.astype(bf16), "ffn_w_out": ffn_w_out.astype(bf16),
        "ln_ffn_g": ln_ffn_g, "ln_ffn_b": ln_ffn_b,
    }
    b, s, d = x_prompt.shape
    y_prompt = _trunk(x_prompt.reshape(b * s, d), b, w).reshape(b, s, d)
    b, s, d = x_sample.shape
    y_sample = _trunk(x_sample.reshape(b * s, d), b, w).reshape(b, s, d)
    return (y_prompt, y_sample)
```

```python
import functools

import numpy as np
import jax
import jax.numpy as jnp
from jax import lax
from jax.experimental import pallas as pl
from jax.experimental.pallas import tpu as pltpu

D_MODEL = 1024
DEPTH = 4
GRID_W = 64
NA_HEADS = 16
HEAD_DIM = D_MODEL // NA_HEADS
MAX_KH = 8
KW = 16
CHUNK = 128
SG_GROUPS = 16
SG_WIDTH = D_MODEL
D_FF = 2816
ALPHA = (2 * DEPTH) ** 0.25
LN_EPS = 1e-5

LANES = 128
SUBLANES = 8
MXU_WIDTH = 256
GROUP_HEADS = MXU_WIDTH // HEAD_DIM
HEAD_GROUPS = D_MODEL // MXU_WIDTH
COL_HALF = GRID_W // 2
ROW_BLOCK = 8
KV_ROWS = 2 * ROW_BLOCK
TOKEN_TILE = 512
SG_TILE, SG_SUB = 512, 256
FFN_TILE, FFN_SUB = 1024, 512
FF_CHUNK = 256
MASK_BIAS = -1e30
LOG2_E = 1.4426950408889634
MIB = 1024 * 1024

f32 = jnp.float32
bf16 = jnp.bfloat16


def _layer_norm(y, g, b):
    mu = jnp.mean(y, axis=-1, keepdims=True)
    yc = y - mu
    var = jnp.mean(yc * yc, axis=-1, keepdims=True)
    return yc * lax.rsqrt(var + LN_EPS) * g + b


def _gelu(z):
    return 0.5 * z * (1.0 + lax.erf(z * (0.5 ** 0.5)))


def _layer_spec(stacked, layer):
    zeros = (0,) * (stacked.ndim - 1)
    return pl.BlockSpec((None,) + stacked.shape[1:], lambda *_: (layer,) + zeros, pipeline_mode=pl.Buffered(1))


def _params(vmem_mib, n_grid=1):
    return pltpu.CompilerParams(
        dimension_semantics=("arbitrary",) * n_grid,
        vmem_limit_bytes=vmem_mib * MIB)


def _qkv_kernel(x_ref, w_ref, o_ref):
    xb = x_ref[...].astype(bf16)
    for c in range(3):
        cols = slice(c * D_MODEL, (c + 1) * D_MODEL)
        acc = jnp.dot(xb, w_ref[:, cols], preferred_element_type=f32)
        if c == 0:
            acc = acc * (HEAD_DIM ** -0.5 * LOG2_E)
        o_ref[:, cols] = acc.astype(bf16)


def _qkv_call(x, w, layer):
    t = x.shape[0]
    return pl.pallas_call(
        _qkv_kernel,
        grid=(t // TOKEN_TILE,),
        in_specs=[pl.BlockSpec((TOKEN_TILE, D_MODEL), lambda i: (i, 0)),
                  _layer_spec(w, layer)],
        out_specs=pl.BlockSpec((TOKEN_TILE, 3 * D_MODEL), lambda i: (i, 0)),
        out_shape=jax.ShapeDtypeStruct((t, 3 * D_MODEL), bf16),
        compiler_params=_params(40),
        name="na_qkv",
    )(x, w)


def _half_key_cols(ch):
    first, last = ch * COL_HALF, (ch + 1) * COL_HALF - 1
    start = lambda c: min(max(c - KW // 2, 0), GRID_W - KW)
    lo = start(first) // SUBLANES * SUBLANES
    hi = -(-(start(last) + KW) // SUBLANES) * SUBLANES
    return lo, hi - lo


def _na_kernel(q_ref, k_ref, v_ref, bias_ref, o_ref, *, rows):
    i = pl.program_id(1)
    wstart = jnp.clip(ROW_BLOCK * i - MAX_KH // 2, 0, rows - KV_ROWS)
    lane_head = lax.broadcasted_iota(jnp.int32, (COL_HALF, MXU_WIDTH), 1) // HEAD_DIM
    nkeys = MAX_KH * GRID_W
    for rr in range(ROW_BLOCK):
        r = ROW_BLOCK * i + rr
        rs = jnp.clip(r - MAX_KH // 2, 0, rows - MAX_KH)
        ks = pl.multiple_of((rs - wstart) * GRID_W, GRID_W)
        bs = pl.multiple_of((rs - r + MAX_KH - 1) * GRID_W, GRID_W)
        for hg in range(HEAD_GROUPS):
            cols = slice(hg * MXU_WIDTH, (hg + 1) * MXU_WIDTH)
            blocks = []
            for ch in range(2):
                qh = q_ref[rr * GRID_W + ch * COL_HALF:rr * GRID_W + (ch + 1) * COL_HALF, cols]
                blocks += [jnp.where(lane_head == h, qh, jnp.zeros_like(qh)) for h in range(GROUP_HEADS)]
            qm = jnp.concatenate(blocks, axis=0)
            kg = k_ref[pl.ds(ks, nkeys), cols]
            vg = v_ref[pl.ds(ks, nkeys), cols]
            s = lax.dot_general(kg, qm, (((1,), (1,)), ((), ())), preferred_element_type=f32)
            halves = []
            for ch in range(2):
                lo, n = _half_key_cols(ch)
                lanes = slice(ch * LANES, (ch + 1) * LANES)
                sc = jnp.concatenate(
                    [s[j * GRID_W + lo:j * GRID_W + lo + n, lanes]
                     + bias_ref[hg, pl.ds(bs + j * GRID_W + lo, n), lanes] for j in range(MAX_KH)], axis=0)
                m = jnp.max(sc, axis=0, keepdims=True)
                p = jnp.exp2(sc - m)
                pn = p * (1.0 / jnp.sum(p, axis=0, keepdims=True))
                pieces = []
                for j in range(MAX_KH):
                    pieces += [jnp.zeros((lo, LANES), f32), pn[j * n:(j + 1) * n],
                               jnp.zeros((GRID_W - lo - n, LANES), f32)]
                halves.append(jnp.concatenate([x for x in pieces if x.shape[0]], axis=0))
            pn = jnp.concatenate(halves, axis=1).astype(bf16)
            pv = lax.dot_general(pn, vg, (((0,), (0,)), ((), ())), preferred_element_type=f32)
            for ch in range(2):
                base = ch * LANES
                out = pv[base:base + COL_HALF]
                for h in range(1, GROUP_HEADS):
                    out = jnp.where(lane_head == h, pv[base + h * COL_HALF:base + (h + 1) * COL_HALF], out)
                o_ref[rr * GRID_W + ch * COL_HALF:rr * GRID_W + (ch + 1) * COL_HALF, cols] = out.astype(bf16)


def _na_call(qkv, bias, layer, n_img, rows):
    t = qkv.shape[0]
    nb = rows // ROW_BLOCK
    blk = ROW_BLOCK * GRID_W
    win = KV_ROWS * GRID_W

    def kv_map(col):
        def index_map(b, i):
            wstart = jnp.clip(ROW_BLOCK * i - MAX_KH // 2, 0, rows - KV_ROWS)
            return ((b * rows + wstart) * GRID_W, col * D_MODEL)
        return index_map

    kv_spec = lambda col: pl.BlockSpec((pl.Element(win), pl.Element(D_MODEL)), kv_map(col))
    return pl.pallas_call(
        functools.partial(_na_kernel, rows=rows),
        grid=(n_img, nb),
        in_specs=[pl.BlockSpec((blk, D_MODEL), lambda b, i: (b * nb + i, 0)),
                  kv_spec(1), kv_spec(2),
                  _layer_spec(bias, layer)],
        out_specs=pl.BlockSpec((blk, D_MODEL), lambda b, i: (b * nb + i, 0)),
        out_shape=jax.ShapeDtypeStruct((t, D_MODEL), bf16),
        compiler_params=_params(48, 2),
        name="na_attn",
    )(qkv, qkv, qkv, bias)


def _na_bias_table(rpb):
    n_dr = 2 * MAX_KH - 1
    c = np.arange(GRID_W)
    col_start = np.clip(c - KW // 2, 0, GRID_W - KW)
    kc = np.arange(GRID_W)
    in_win = (kc[:, None] >= col_start[None, :]) & (kc[:, None] < col_start[None, :] + KW)
    e = np.arange(2 * KW - 1)
    select = (e[:, None, None] == (kc[:, None] - c[None, :] + KW - 1)[None]) & in_win[None]
    t = jnp.einsum("hde,ekc->hdkc", rpb.astype(f32), jnp.asarray(select, f32),
                   precision=lax.Precision.HIGHEST)
    t = jnp.where(in_win[None, None], t * LOG2_E, MASK_BIAS)
    t = t.reshape(HEAD_GROUPS, GROUP_HEADS, n_dr, GRID_W, 2, COL_HALF)
    t = t.transpose(0, 2, 3, 4, 1, 5)
    return t.reshape(HEAD_GROUPS, n_dr * GRID_W, MXU_WIDTH)


def _sg_kernel(x_ref, win_ref, lng_ref, lnb_ref, wsp_ref, bsp_ref, wout_ref, g_ref, b_ref,
               y_ref, gated_ref):
    lane = lax.broadcasted_iota(jnp.int32, (CHUNK, LANES), 1)
    first_group = lane < SG_WIDTH // SG_GROUPS
    zero = jnp.zeros((CHUNK, LANES), bf16)
    subs = [slice(s * SG_SUB, (s + 1) * SG_SUB) for s in range(SG_TILE // SG_SUB)]
    raw = []
    for rows in subs:
        xb = x_ref[rows, :].astype(bf16)
        raw.append((jnp.dot(xb, win_ref[:, :SG_WIDTH], preferred_element_type=f32),
                    jnp.dot(xb, win_ref[:, SG_WIDTH:], preferred_element_type=f32)))
    act = [(_gelu(u), _layer_norm(_gelu(v), lng_ref[...], lnb_ref[...]).astype(bf16)) for u, v in raw]
    mixed = []
    for rows, (u, vb) in zip(subs, act):
        for n in range(SG_SUB // CHUNK):
            toks = slice(n * CHUNK, (n + 1) * CHUNK)
            for j in range(SG_GROUPS // 2):
                cols = slice(j * LANES, (j + 1) * LANES)
                vp = vb[toks, cols]
                rhs = jnp.concatenate([jnp.where(first_group, vp, zero),
                                       jnp.where(first_group, zero, vp)], axis=0)
                sp = jnp.dot(wsp_ref[j], rhs, preferred_element_type=f32) + bsp_ref[j]
                gated_ref[rows.start + n * CHUNK:rows.start + (n + 1) * CHUNK, cols] = (
                    u[toks, cols] * sp).astype(bf16)
        mixed.append(jnp.dot(gated_ref[rows, :], wout_ref[...], preferred_element_type=f32))
    for rows, m in zip(subs, mixed):
        y_ref[rows, :] = _layer_norm(ALPHA * x_ref[rows, :] + m, g_ref[...], b_ref[...])


def _sg_call(x, params, layer, g, b, ln_layer):
    t = x.shape[0]
    tile = lambda: pl.BlockSpec((SG_TILE, D_MODEL), lambda i: (i, 0))
    return pl.pallas_call(
        _sg_kernel,
        grid=(t // SG_TILE,),
        in_specs=[tile()] + [_layer_spec(p, layer) for p in params]
                 + [_layer_spec(g, ln_layer), _layer_spec(b, ln_layer)],
        out_specs=tile(),
        out_shape=jax.ShapeDtypeStruct((t, D_MODEL), f32),
        scratch_shapes=[pltpu.VMEM((SG_TILE, SG_WIDTH), bf16)],
        compiler_params=_params(48),
        name="sg_layer",
    )(x, *params, g, b)


def _ffn_kernel(*refs, with_proj):
    if with_proj:
        x_ref, a_ref, wp_ref, gm_ref, bm_ref, win_ref, wout_ref, g_ref, b_ref, y_ref, acc_ref, xb_ref = refs
        src_ref = y_ref
    else:
        x_ref, win_ref, wout_ref, g_ref, b_ref, y_ref, acc_ref, xb_ref = refs
        src_ref = x_ref
    subs = [slice(s * FFN_SUB, (s + 1) * FFN_SUB) for s in range(FFN_TILE // FFN_SUB)]
    n_chunks = D_FF // FF_CHUNK

    def prologue(rows):
        if with_proj:
            m = jnp.dot(a_ref[rows, :], wp_ref[...], preferred_element_type=f32)
            y_ref[rows, :] = _layer_norm(ALPHA * x_ref[rows, :] + m, gm_ref[...], bm_ref[...])
        xb_ref[rows, :] = src_ref[rows, :].astype(bf16)

    def chunk(rows, c):
        xb = xb_ref[rows, :]
        gate = jnp.dot(xb, win_ref[:, c * FF_CHUNK:(c + 1) * FF_CHUNK], preferred_element_type=f32)
        lin = jnp.dot(xb, win_ref[:, D_FF + c * FF_CHUNK:D_FF + (c + 1) * FF_CHUNK],
                      preferred_element_type=f32)
        act = (jax.nn.silu(gate) * lin).astype(bf16)
        part = jnp.dot(act, wout_ref[c * FF_CHUNK:(c + 1) * FF_CHUNK, :], preferred_element_type=f32)
        if c == 0:
            acc_ref[rows, :] = part
        else:
            acc_ref[rows, :] += part

    def epilogue(rows):
        y_ref[rows, :] = _layer_norm(ALPHA * src_ref[rows, :] + acc_ref[rows, :], g_ref[...], b_ref[...])

    prologue(subs[0])
    for s, rows in enumerate(subs):
        for c in range(n_chunks):
            chunk(rows, c)
            if c == n_chunks // 2 and s + 1 < len(subs):
                prologue(subs[s + 1])
        epilogue(rows)


def _ffn_call(x, params, layer, proj=None):
    t = x.shape[0]
    tile = lambda: pl.BlockSpec((FFN_TILE, D_MODEL), lambda i: (i, 0))
    in_specs, args = [tile()], [x]
    if proj is not None:
        attn, (wp, wp_layer), (gm, bm, ln_layer) = proj
        in_specs += [tile(), _layer_spec(wp, wp_layer), _layer_spec(gm, ln_layer), _layer_spec(bm, ln_layer)]
        args += [attn, wp, gm, bm]
    in_specs += [_layer_spec(p, layer) for p in params]
    args += list(params)
    return pl.pallas_call(
        functools.partial(_ffn_kernel, with_proj=proj is not None),
        grid=(t // FFN_TILE,),
        in_specs=in_specs,
        out_specs=tile(),
        out_shape=jax.ShapeDtypeStruct((t, D_MODEL), f32),
        scratch_shapes=[pltpu.VMEM((FFN_TILE, D_MODEL), f32), pltpu.VMEM((FFN_TILE, D_MODEL), bf16)],
        compiler_params=_params(56),
        name="ffn_proj" if proj is not None else "ffn",
    )(*args)


def _trunk(x, n_img, w):
    t = x.shape[0]
    rows = t // (n_img * GRID_W)
    for i in range(DEPTH):
        j = i // 2
        if i % 2 == 0:
            qkv = _qkv_call(x, w["na_w_in"], j)
            attn = _na_call(qkv, w["na_bias"], j, n_img, rows)
            proj = (attn, (w["na_w_out"], j), (w["ln_mix_g"], w["ln_mix_b"], i))
        else:
            x = _sg_call(x, w["sg"], j, w["ln_mix_g"], w["ln_mix_b"], i)
            proj = None
        x = _ffn_call(x, w["ffn"], i, proj)
    return x


def kernel(x_prompt, x_sample, na_w_in, na_rpb, na_w_out, sg_w_in, sg_ln_g, sg_ln_b, sg_w_s, sg_b_s,
           sg_w_out, ln_mix_g, ln_mix_b, ffn_w_in, ffn_w_out, ln_ffn_g, ln_ffn_b):
    n_sg = sg_w_s.shape[0]
    group_dim = SG_WIDTH // SG_GROUPS
    vec = lambda a: a.reshape(a.shape[0], 1, a.shape[1])
    w = {
        "na_w_in": na_w_in.astype(bf16),
        "na_bias": jnp.stack([_na_bias_table(na_rpb[j]) for j in range(na_rpb.shape[0])]),
        "na_w_out": na_w_out.astype(bf16),
        "sg": (
            sg_w_in.astype(bf16), vec(sg_ln_g), vec(sg_ln_b),
            sg_w_s.astype(bf16).reshape(n_sg, SG_GROUPS // 2, 2, CHUNK, CHUNK)
            .transpose(0, 1, 3, 2, 4).reshape(n_sg, SG_GROUPS // 2, CHUNK, 2 * CHUNK),
            jnp.repeat(sg_b_s.reshape(n_sg, SG_GROUPS // 2, 2, CHUNK).transpose(0, 1, 3, 2), group_dim, axis=-1),
            sg_w_out.astype(bf16)),
        "ln_mix_g": vec(ln_mix_g), "ln_mix_b": vec(ln_mix_b),
        "ffn": (ffn_w_in.astype(bf16), ffn_w_out.astype(bf16), vec(ln_ffn_g), vec(ln_ffn_b)),
    }
    b, s, d = x_prompt.shape
    y_prompt = _trunk(x_prompt.reshape(b * s, d), b, w).reshape(b, s, d)
    b, s, d = x_sample.shape
    y_sample = _trunk(x_sample.reshape(b * s, d), b, w).reshape(b, s, d)
    return (y_prompt, y_sample)
```

```python
import functools

import numpy as np
import jax
import jax.numpy as jnp
from jax import lax
from jax.experimental import pallas as pl
from jax.experimental.pallas import tpu as pltpu

D_MODEL = 1024
DEPTH = 4
GRID_W = 64
NA_HEADS = 16
HEAD_DIM = D_MODEL // NA_HEADS
MAX_KH = 8
KW = 16
CHUNK = 128
SG_GROUPS = 16
SG_WIDTH = D_MODEL
D_FF = 2816
ALPHA = (2 * DEPTH) ** 0.25
LN_EPS = 1e-5

LANES = 128
SUBLANES = 8
MXU_WIDTH = 256
GROUP_HEADS = MXU_WIDTH // HEAD_DIM
HEAD_GROUPS = D_MODEL // MXU_WIDTH
COL_HALF = GRID_W // 2
ROW_BLOCK = 8
KV_ROWS = 2 * ROW_BLOCK
TOKEN_TILE = 512
SG_TILE, SG_SUB = 512, 256
FFN_TILE, FFN_SUB = 1024, 512
FF_CHUNK = 256
MASK_BIAS = -1e30
LOG2_E = 1.4426950408889634
MIB = 1024 * 1024

f32 = jnp.float32
bf16 = jnp.bfloat16


def _layer_norm(y, g, b):
    mu = jnp.mean(y, axis=-1, keepdims=True)
    yc = y - mu
    var = jnp.mean(yc * yc, axis=-1, keepdims=True)
    return yc * lax.rsqrt(var + LN_EPS) * g + b


def _gelu(z):
    return 0.5 * z * (1.0 + lax.erf(z * (0.5 ** 0.5)))


def _layer_spec(stacked, layer):
    zeros = (0,) * (stacked.ndim - 1)
    return pl.BlockSpec((None,) + stacked.shape[1:], lambda *_: (layer,) + zeros, pipeline_mode=pl.Buffered(1))


def _params(vmem_mib, n_grid=1):
    return pltpu.CompilerParams(
        dimension_semantics=("arbitrary",) * n_grid,
        vmem_limit_bytes=vmem_mib * MIB)


def _qkv_kernel(x_ref, w_ref, o_ref):
    xb = x_ref[...].astype(bf16)
    for c in range(3):
        cols = slice(c * D_MODEL, (c + 1) * D_MODEL)
        acc = jnp.dot(xb, w_ref[:, cols], preferred_element_type=f32)
        if c == 0:
            acc = acc * (HEAD_DIM ** -0.5 * LOG2_E)
        o_ref[:, cols] = acc.astype(bf16)


def _qkv_call(x, w, layer):
    t = x.shape[0]
    return pl.pallas_call(
        _qkv_kernel,
        grid=(t // TOKEN_TILE,),
        in_specs=[pl.BlockSpec((TOKEN_TILE, D_MODEL), lambda i: (i, 0)),
                  _layer_spec(w, layer)],
        out_specs=pl.BlockSpec((TOKEN_TILE, 3 * D_MODEL), lambda i: (i, 0)),
        out_shape=jax.ShapeDtypeStruct((t, 3 * D_MODEL), bf16),
        compiler_params=_params(40),
        name="na_qkv",
    )(x, w)


def _half_key_cols(ch):
    first, last = ch * COL_HALF, (ch + 1) * COL_HALF - 1
    start = lambda c: min(max(c - KW // 2, 0), GRID_W - KW)
    lo = start(first) // SUBLANES * SUBLANES
    hi = -(-(start(last) + KW) // SUBLANES) * SUBLANES
    return lo, hi - lo


def _na_kernel(q_ref, k_ref, v_ref, bias_ref, o_ref, *, rows):
    i = pl.program_id(1)
    wstart = jnp.clip(ROW_BLOCK * i - MAX_KH // 2, 0, rows - KV_ROWS)
    lane_head = lax.broadcasted_iota(jnp.int32, (COL_HALF, MXU_WIDTH), 1) // HEAD_DIM
    nkeys = MAX_KH * GRID_W
    for rr in range(ROW_BLOCK):
        r = ROW_BLOCK * i + rr
        rs = jnp.clip(r - MAX_KH // 2, 0, rows - MAX_KH)
        ks = pl.multiple_of((rs - wstart) * GRID_W, GRID_W)
        bs = pl.multiple_of((rs - r + MAX_KH - 1) * GRID_W, GRID_W)
        for hg in range(HEAD_GROUPS):
            cols = slice(hg * MXU_WIDTH, (hg + 1) * MXU_WIDTH)
            blocks = []
            for ch in range(2):
                qh = q_ref[rr * GRID_W + ch * COL_HALF:rr * GRID_W + (ch + 1) * COL_HALF, cols]
                blocks += [jnp.where(lane_head == h, qh, jnp.zeros_like(qh)) for h in range(GROUP_HEADS)]
            qm = jnp.concatenate(blocks, axis=0)
            kg = k_ref[pl.ds(ks, nkeys), cols]
            vg = v_ref[pl.ds(ks, nkeys), cols]
            s = lax.dot_general(kg, qm, (((1,), (1,)), ((), ())), preferred_element_type=f32)
            halves = []
            for ch in range(2):
                lo, n = _half_key_cols(ch)
                lanes = slice(ch * LANES, (ch + 1) * LANES)
                sc = jnp.concatenate(
                    [s[j * GRID_W + lo:j * GRID_W + lo + n, lanes]
                     + bias_ref[hg, pl.ds(bs + j * GRID_W + lo, n), lanes] for j in range(MAX_KH)], axis=0)
                m = jnp.max(sc, axis=0, keepdims=True)
                p = jnp.exp2(sc - m)
                pn = p * (1.0 / jnp.sum(p, axis=0, keepdims=True))
                pieces = []
                for j in range(MAX_KH):
                    pieces += [jnp.zeros((lo, LANES), f32), pn[j * n:(j + 1) * n],
                               jnp.zeros((GRID_W - lo - n, LANES), f32)]
                halves.append(jnp.concatenate([x for x in pieces if x.shape[0]], axis=0))
            pn = jnp.concatenate(halves, axis=1).astype(bf16)
            pv = lax.dot_general(pn, vg, (((0,), (0,)), ((), ())), preferred_element_type=f32)
            for ch in range(2):
                base = ch * LANES
                out = pv[base:base + COL_HALF]
                for h in range(1, GROUP_HEADS):
                    out = jnp.where(lane_head == h, pv[base + h * COL_HALF:base + (h + 1) * COL_HALF], out)
                o_ref[rr * GRID_W + ch * COL_HALF:rr * GRID_W + (ch + 1) * COL_HALF, cols] = out.astype(bf16)


def _na_call(qkv, bias, layer, n_img, rows):
    t = qkv.shape[0]
    nb = rows // ROW_BLOCK
    blk = ROW_BLOCK * GRID_W
    win = KV_ROWS * GRID_W

    def kv_map(col):
        def index_map(b, i):
            wstart = jnp.clip(ROW_BLOCK * i - MAX_KH // 2, 0, rows - KV_ROWS)
            return ((b * rows + wstart) * GRID_W, col * D_MODEL)
        return index_map

    kv_spec = lambda col: pl.BlockSpec((pl.Element(win), pl.Element(D_MODEL)), kv_map(col))
    return pl.pallas_call(
        functools.partial(_na_kernel, rows=rows),
        grid=(n_img, nb),
        in_specs=[pl.BlockSpec((blk, D_MODEL), lambda b, i: (b * nb + i, 0)),
                  kv_spec(1), kv_spec(2),
                  _layer_spec(bias, layer)],
        out_specs=pl.BlockSpec((blk, D_MODEL), lambda b, i: (b * nb + i, 0)),
        out_shape=jax.ShapeDtypeStruct((t, D_MODEL), bf16),
        compiler_params=_params(48, 2),
        name="na_attn",
    )(qkv, qkv, qkv, bias)


def _na_bias_tables(rpb):
    n_layers = rpb.shape[0]
    n_dr, n_dc = 2 * MAX_KH - 1, 2 * KW - 1
    c = (np.arange(2)[:, None, None] * COL_HALF + np.arange(COL_HALF)[None, None, :]
         + np.zeros((1, GROUP_HEADS, 1), np.int64)).reshape(-1)
    hh = (np.zeros((2, 1, COL_HALF), np.int64) + np.arange(GROUP_HEADS)[None, :, None]).reshape(-1)
    col_start = np.clip(c - KW // 2, 0, GRID_W - KW)
    kc = np.arange(GRID_W)
    in_win = (kc[:, None] >= col_start[None, :]) & (kc[:, None] < col_start[None, :] + KW)
    e = np.arange(n_dc)
    select = ((e[:, None, None, None] == (kc[:, None] - c[None, :] + KW - 1)[None, None])
              & (np.arange(GROUP_HEADS)[None, :, None, None] == hh[None, None, None, :]) & in_win[None, None])
    select = select.reshape(n_dc * GROUP_HEADS, GRID_W * MXU_WIDTH).astype(np.float32)
    lhs = rpb.astype(f32).reshape(n_layers, HEAD_GROUPS, GROUP_HEADS, n_dr, n_dc)
    lhs = lhs.transpose(0, 1, 3, 4, 2).reshape(n_layers * HEAD_GROUPS * n_dr, n_dc * GROUP_HEADS)
    t = jnp.dot(lhs, select, precision=lax.Precision.HIGHEST)
    t = t.reshape(n_layers, HEAD_GROUPS, n_dr, GRID_W, MXU_WIDTH)
    t = jnp.where(in_win[None, None, None], t * LOG2_E, MASK_BIAS)
    return t.reshape(n_layers, HEAD_GROUPS, n_dr * GRID_W, MXU_WIDTH)


def _sg_kernel(x_ref, win_ref, lng_ref, lnb_ref, wsp_ref, bsp_ref, wout_ref, g_ref, b_ref,
               y_ref, gated_ref):
    lane = lax.broadcasted_iota(jnp.int32, (CHUNK, LANES), 1)
    first_group = lane < SG_WIDTH // SG_GROUPS
    zero = jnp.zeros((CHUNK, LANES), bf16)
    subs = [slice(s * SG_SUB, (s + 1) * SG_SUB) for s in range(SG_TILE // SG_SUB)]
    raw = []
    for rows in subs:
        xb = x_ref[rows, :].astype(bf16)
        raw.append((jnp.dot(xb, win_ref[:, :SG_WIDTH], preferred_element_type=f32),
                    jnp.dot(xb, win_ref[:, SG_WIDTH:], preferred_element_type=f32)))
    act = [(_gelu(u), _layer_norm(_gelu(v), lng_ref[...], lnb_ref[...]).astype(bf16)) for u, v in raw]
    mixed = []
    for rows, (u, vb) in zip(subs, act):
        for n in range(SG_SUB // CHUNK):
            toks = slice(n * CHUNK, (n + 1) * CHUNK)
            for j in range(SG_GROUPS // 2):
                cols = slice(j * LANES, (j + 1) * LANES)
                vp = vb[toks, cols]
                rhs = jnp.concatenate([jnp.where(first_group, vp, zero),
                                       jnp.where(first_group, zero, vp)], axis=0)
                sp = jnp.dot(wsp_ref[j], rhs, preferred_element_type=f32) + bsp_ref[j]
                gated_ref[rows.start + n * CHUNK:rows.start + (n + 1) * CHUNK, cols] = (
                    u[toks, cols] * sp).astype(bf16)
        mixed.append(jnp.dot(gated_ref[rows, :], wout_ref[...], preferred_element_type=f32))
    for rows, m in zip(subs, mixed):
        y_ref[rows, :] = _layer_norm(ALPHA * x_ref[rows, :] + m, g_ref[...], b_ref[...])


def _sg_call(x, params, layer, g, b, ln_layer):
    t = x.shape[0]
    tile = lambda: pl.BlockSpec((SG_TILE, D_MODEL), lambda i: (i, 0))
    return pl.pallas_call(
        _sg_kernel,
        grid=(t // SG_TILE,),
        in_specs=[tile()] + [_layer_spec(p, layer) for p in params]
                 + [_layer_spec(g, ln_layer), _layer_spec(b, ln_layer)],
        out_specs=tile(),
        out_shape=jax.ShapeDtypeStruct((t, D_MODEL), f32),
        scratch_shapes=[pltpu.VMEM((SG_TILE, SG_WIDTH), bf16)],
        compiler_params=_params(48),
        name="sg_layer",
    )(x, *params, g, b)


def _ffn_kernel(*refs, with_proj):
    if with_proj:
        x_ref, a_ref, wp_ref, gm_ref, bm_ref, win_ref, wout_ref, g_ref, b_ref, y_ref, act_ref, xb_ref = refs
        src_ref = y_ref
    else:
        x_ref, win_ref, wout_ref, g_ref, b_ref, y_ref, act_ref, xb_ref = refs
        src_ref = x_ref
    subs = [slice(s * FFN_SUB, (s + 1) * FFN_SUB) for s in range(FFN_TILE // FFN_SUB)]
    n_chunks = D_FF // FF_CHUNK

    def prologue(rows):
        if with_proj:
            m = jnp.dot(a_ref[rows, :], wp_ref[...], preferred_element_type=f32)
            y_ref[rows, :] = _layer_norm(ALPHA * x_ref[rows, :] + m, gm_ref[...], bm_ref[...])
        xb_ref[rows, :] = src_ref[rows, :].astype(bf16)

    def chunk(rows, c):
        xb = xb_ref[rows, :]
        gate = jnp.dot(xb, win_ref[:, c * FF_CHUNK:(c + 1) * FF_CHUNK], preferred_element_type=f32)
        lin = jnp.dot(xb, win_ref[:, D_FF + c * FF_CHUNK:D_FF + (c + 1) * FF_CHUNK],
                      preferred_element_type=f32)
        act_ref[rows, c * FF_CHUNK:(c + 1) * FF_CHUNK] = (jax.nn.silu(gate) * lin).astype(bf16)

    def epilogue(rows):
        down = jnp.dot(act_ref[rows, :], wout_ref[...], preferred_element_type=f32)
        y_ref[rows, :] = _layer_norm(ALPHA * src_ref[rows, :] + down, g_ref[...], b_ref[...])

    prologue(subs[0])
    for s, rows in enumerate(subs):
        for c in range(n_chunks):
            chunk(rows, c)
            if c == n_chunks // 2 and s + 1 < len(subs):
                prologue(subs[s + 1])
        epilogue(rows)


def _ffn_call(x, params, layer, proj=None):
    t = x.shape[0]
    tile = lambda: pl.BlockSpec((FFN_TILE, D_MODEL), lambda i: (i, 0))
    in_specs, args = [tile()], [x]
    if proj is not None:
        attn, (wp, wp_layer), (gm, bm, ln_layer) = proj
        in_specs += [tile(), _layer_spec(wp, wp_layer), _layer_spec(gm, ln_layer), _layer_spec(bm, ln_layer)]
        args += [attn, wp, gm, bm]
    in_specs += [_layer_spec(p, layer) for p in params]
    args += list(params)
    return pl.pallas_call(
        functools.partial(_ffn_kernel, with_proj=proj is not None),
        grid=(t // FFN_TILE,),
        in_specs=in_specs,
        out_specs=tile(),
        out_shape=jax.ShapeDtypeStruct((t, D_MODEL), f32),
        scratch_shapes=[pltpu.VMEM((FFN_TILE, D_FF), bf16), pltpu.VMEM((FFN_TILE, D_MODEL), bf16)],
        compiler_params=_params(56),
        name="ffn_proj" if proj is not None else "ffn",
    )(*args)


def _trunk(x, n_img, w):
    t = x.shape[0]
    rows = t // (n_img * GRID_W)
    for i in range(DEPTH):
        j = i // 2
        if i % 2 == 0:
            qkv = _qkv_call(x, w["na_w_in"], j)
            attn = _na_call(qkv, w["na_bias"], j, n_img, rows)
            proj = (attn, (w["na_w_out"], j), (w["ln_mix_g"], w["ln_mix_b"], i))
        else:
            x = _sg_call(x, w["sg"], j, w["ln_mix_g"], w["ln_mix_b"], i)
            proj = None
        x = _ffn_call(x, w["ffn"], i, proj)
    return x


def kernel(x_prompt, x_sample, na_w_in, na_rpb, na_w_out, sg_w_in, sg_ln_g, sg_ln_b, sg_w_s, sg_b_s,
           sg_w_out, ln_mix_g, ln_mix_b, ffn_w_in, ffn_w_out, ln_ffn_g, ln_ffn_b):
    n_sg = sg_w_s.shape[0]
    group_dim = SG_WIDTH // SG_GROUPS
    vec = lambda a: a.reshape(a.shape[0], 1, a.shape[1])
    w = {
        "na_w_in": na_w_in.astype(bf16),
        "na_bias": _na_bias_tables(na_rpb),
        "na_w_out": na_w_out.astype(bf16),
        "sg": (
            sg_w_in.astype(bf16), vec(sg_ln_g), vec(sg_ln_b),
            sg_w_s.astype(bf16).reshape(n_sg, SG_GROUPS // 2, 2, CHUNK, CHUNK)
            .transpose(0, 1, 3, 2, 4).reshape(n_sg, SG_GROUPS // 2, CHUNK, 2 * CHUNK),
            jnp.repeat(sg_b_s.reshape(n_sg, SG_GROUPS // 2, 2, CHUNK).transpose(0, 1, 3, 2), group_dim, axis=-1),
            sg_w_out.astype(bf16)),
        "ln_mix_g": vec(ln_mix_g), "ln_mix_b": vec(ln_mix_b),
        "ffn": (ffn_w_in.astype(bf16), ffn_w_out.astype(bf16), vec(ln_ffn_g), vec(ln_ffn_b)),
    }
    b, s, d = x_prompt.shape
    y_prompt = _trunk(x_prompt.reshape(b * s, d), b, w).reshape(b, s, d)
    b, s, d = x_sample.shape
    y_sample = _trunk(x_sample.reshape(b * s, d), b, w).reshape(b, s, d)
    return (y_prompt, y_sample)
```

```python
import functools

import numpy as np
import jax
import jax.numpy as jnp
from jax import lax
from jax.experimental import pallas as pl
from jax.experimental.pallas import tpu as pltpu

D_MODEL = 1024
DEPTH = 4
GRID_W = 64
NA_HEADS = 16
HEAD_DIM = D_MODEL // NA_HEADS
MAX_KH = 8
KW = 16
CHUNK = 128
SG_GROUPS = 16
SG_WIDTH = D_MODEL
D_FF = 2816
ALPHA = (2 * DEPTH) ** 0.25
LN_EPS = 1e-5

LANES = 128
SUBLANES = 8
MXU_WIDTH = 256
GROUP_HEADS = MXU_WIDTH // HEAD_DIM
HEAD_GROUPS = D_MODEL // MXU_WIDTH
COL_HALF = GRID_W // 2
ROW_BLOCK = 16
KV_ROWS = ROW_BLOCK + MAX_KH
TOKEN_TILE = 1024
SG_TILE, SG_SUB = 1024, 256
FFN_TILE, FFN_SUB = 1024, 512
FF_CHUNK = 256
MASK_BIAS = -1e30
LOG2_E = 1.4426950408889634
MIB = 1024 * 1024

f32 = jnp.float32
bf16 = jnp.bfloat16


def _layer_norm(y, g, b):
    mu = jnp.mean(y, axis=-1, keepdims=True)
    yc = y - mu
    var = jnp.mean(yc * yc, axis=-1, keepdims=True)
    return yc * lax.rsqrt(var + LN_EPS) * g + b


def _gelu(z):
    return 0.5 * z * (1.0 + lax.erf(z * (0.5 ** 0.5)))


def _layer_spec(stacked, layer):
    zeros = (0,) * (stacked.ndim - 1)
    return pl.BlockSpec((None,) + stacked.shape[1:], lambda *_: (layer,) + zeros, pipeline_mode=pl.Buffered(1))


def _params(vmem_mib, n_grid=1):
    return pltpu.CompilerParams(
        dimension_semantics=("arbitrary",) * n_grid,
        vmem_limit_bytes=vmem_mib * MIB)


def _qkv_kernel(x_ref, w_ref, o_ref):
    xb = x_ref[...].astype(bf16)
    for c in range(3):
        cols = slice(c * D_MODEL, (c + 1) * D_MODEL)
        acc = jnp.dot(xb, w_ref[:, cols], preferred_element_type=f32)
        if c == 0:
            acc = acc * (HEAD_DIM ** -0.5 * LOG2_E)
        o_ref[:, cols] = acc.astype(bf16)


def _qkv_call(x, w, layer):
    t = x.shape[0]
    return pl.pallas_call(
        _qkv_kernel,
        grid=(t // TOKEN_TILE,),
        in_specs=[pl.BlockSpec((TOKEN_TILE, D_MODEL), lambda i: (i, 0)),
                  _layer_spec(w, layer)],
        out_specs=pl.BlockSpec((TOKEN_TILE, 3 * D_MODEL), lambda i: (i, 0)),
        out_shape=jax.ShapeDtypeStruct((t, 3 * D_MODEL), bf16),
        compiler_params=_params(40),
        name="na_qkv",
    )(x, w)


def _half_key_cols(ch):
    first, last = ch * COL_HALF, (ch + 1) * COL_HALF - 1
    start = lambda c: min(max(c - KW // 2, 0), GRID_W - KW)
    lo = start(first) // SUBLANES * SUBLANES
    hi = -(-(start(last) + KW) // SUBLANES) * SUBLANES
    return lo, hi - lo


def _na_kernel(q_ref, k_ref, v_ref, bias_ref, o_ref, *, rows):
    i = pl.program_id(1)
    wstart = jnp.clip(ROW_BLOCK * i - MAX_KH // 2, 0, rows - KV_ROWS)
    lane_head = lax.broadcasted_iota(jnp.int32, (COL_HALF, MXU_WIDTH), 1) // HEAD_DIM
    nkeys = MAX_KH * GRID_W
    for rr in range(ROW_BLOCK):
        r = ROW_BLOCK * i + rr
        rs = jnp.clip(r - MAX_KH // 2, 0, rows - MAX_KH)
        ks = pl.multiple_of((rs - wstart) * GRID_W, GRID_W)
        bs = pl.multiple_of((rs - r + MAX_KH - 1) * GRID_W, GRID_W)
        for hg in range(HEAD_GROUPS):
            cols = slice(hg * MXU_WIDTH, (hg + 1) * MXU_WIDTH)
            blocks = []
            for ch in range(2):
                qh = q_ref[rr * GRID_W + ch * COL_HALF:rr * GRID_W + (ch + 1) * COL_HALF, cols]
                blocks += [jnp.where(lane_head == h, qh, jnp.zeros_like(qh)) for h in range(GROUP_HEADS)]
            qm = jnp.concatenate(blocks, axis=0)
            kg = k_ref[pl.ds(ks, nkeys), cols]
            vg = v_ref[pl.ds(ks, nkeys), cols]
            s = lax.dot_general(kg, qm, (((1,), (1,)), ((), ())), preferred_element_type=f32)
            halves = []
            for ch in range(2):
                lo, n = _half_key_cols(ch)
                lanes = slice(ch * LANES, (ch + 1) * LANES)
                sc = jnp.concatenate(
                    [s[j * GRID_W + lo:j * GRID_W + lo + n, lanes]
                     + bias_ref[hg, pl.ds(bs + j * GRID_W + lo, n), lanes] for j in range(MAX_KH)], axis=0)
                m = jnp.max(sc, axis=0, keepdims=True)
                p = jnp.exp2(sc - m)
                pn = p * (1.0 / jnp.sum(p, axis=0, keepdims=True))
                pieces = []
                for j in range(MAX_KH):
                    pieces += [jnp.zeros((lo, LANES), f32), pn[j * n:(j + 1) * n],
                               jnp.zeros((GRID_W - lo - n, LANES), f32)]
                halves.append(jnp.concatenate([x for x in pieces if x.shape[0]], axis=0))
            pn = jnp.concatenate(halves, axis=1).astype(bf16)
            pv = lax.dot_general(pn, vg, (((0,), (0,)), ((), ())), preferred_element_type=f32)
            for ch in range(2):
                base = ch * LANES
                out = pv[base:base + COL_HALF]
                for h in range(1, GROUP_HEADS):
                    out = jnp.where(lane_head == h, pv[base + h * COL_HALF:base + (h + 1) * COL_HALF], out)
                o_ref[rr * GRID_W + ch * COL_HALF:rr * GRID_W + (ch + 1) * COL_HALF, cols] = out.astype(bf16)


def _na_call(qkv, bias, layer, n_img, rows):
    t = qkv.shape[0]
    nb = rows // ROW_BLOCK
    blk = ROW_BLOCK * GRID_W
    win = KV_ROWS * GRID_W

    def kv_map(col):
        def index_map(b, i):
            wstart = jnp.clip(ROW_BLOCK * i - MAX_KH // 2, 0, rows - KV_ROWS)
            return ((b * rows + wstart) * GRID_W, col * D_MODEL)
        return index_map

    kv_spec = lambda col: pl.BlockSpec((pl.Element(win), pl.Element(D_MODEL)), kv_map(col))
    return pl.pallas_call(
        functools.partial(_na_kernel, rows=rows),
        grid=(n_img, nb),
        in_specs=[pl.BlockSpec((blk, D_MODEL), lambda b, i: (b * nb + i, 0)),
                  kv_spec(1), kv_spec(2),
                  _layer_spec(bias, layer)],
        out_specs=pl.BlockSpec((blk, D_MODEL), lambda b, i: (b * nb + i, 0)),
        out_shape=jax.ShapeDtypeStruct((t, D_MODEL), bf16),
        compiler_params=_params(48, 2),
        name="na_attn",
    )(qkv, qkv, qkv, bias)


def _na_bias_tables(rpb):
    n_layers = rpb.shape[0]
    n_dr, n_dc = 2 * MAX_KH - 1, 2 * KW - 1
    c = (np.arange(2)[:, None, None] * COL_HALF + np.arange(COL_HALF)[None, None, :]
         + np.zeros((1, GROUP_HEADS, 1), np.int64)).reshape(-1)
    hh = (np.zeros((2, 1, COL_HALF), np.int64) + np.arange(GROUP_HEADS)[None, :, None]).reshape(-1)
    col_start = np.clip(c - KW // 2, 0, GRID_W - KW)
    kc = np.arange(GRID_W)
    in_win = (kc[:, None] >= col_start[None, :]) & (kc[:, None] < col_start[None, :] + KW)
    e = np.arange(n_dc)
    select = ((e[:, None, None, None] == (kc[:, None] - c[None, :] + KW - 1)[None, None])
              & (np.arange(GROUP_HEADS)[None, :, None, None] == hh[None, None, None, :]) & in_win[None, None])
    select = select.reshape(n_dc * GROUP_HEADS, GRID_W * MXU_WIDTH).astype(np.float32)
    lhs = rpb.astype(f32).reshape(n_layers, HEAD_GROUPS, GROUP_HEADS, n_dr, n_dc)
    lhs = lhs.transpose(0, 1, 3, 4, 2).reshape(n_layers * HEAD_GROUPS * n_dr, n_dc * GROUP_HEADS)
    t = jnp.dot(lhs, select, precision=lax.Precision.HIGHEST)
    t = t.reshape(n_layers, HEAD_GROUPS, n_dr, GRID_W, MXU_WIDTH)
    t = jnp.where(in_win[None, None, None], t * LOG2_E, MASK_BIAS)
    return t.reshape(n_layers, HEAD_GROUPS, n_dr * GRID_W, MXU_WIDTH)


def _sg_kernel(x_ref, win_ref, lng_ref, lnb_ref, wsp_ref, bsp_ref, wout_ref, g_ref, b_ref,
               y_ref, gated_ref):
    lane = lax.broadcasted_iota(jnp.int32, (CHUNK, LANES), 1)
    first_group = lane < SG_WIDTH // SG_GROUPS
    zero = jnp.zeros((CHUNK, LANES), bf16)
    subs = [slice(s * SG_SUB, (s + 1) * SG_SUB) for s in range(SG_TILE // SG_SUB)]
    raw = []
    for rows in subs:
        xb = x_ref[rows, :].astype(bf16)
        raw.append((jnp.dot(xb, win_ref[:, :SG_WIDTH], preferred_element_type=f32),
                    jnp.dot(xb, win_ref[:, SG_WIDTH:], preferred_element_type=f32)))
    act = [(_gelu(u), _layer_norm(_gelu(v), lng_ref[...], lnb_ref[...]).astype(bf16)) for u, v in raw]
    mixed = []
    for rows, (u, vb) in zip(subs, act):
        for n in range(SG_SUB // CHUNK):
            toks = slice(n * CHUNK, (n + 1) * CHUNK)
            for j in range(SG_GROUPS // 2):
                cols = slice(j * LANES, (j + 1) * LANES)
                vp = vb[toks, cols]
                rhs = jnp.concatenate([jnp.where(first_group, vp, zero),
                                       jnp.where(first_group, zero, vp)], axis=0)
                sp = jnp.dot(wsp_ref[j], rhs, preferred_element_type=f32) + bsp_ref[j]
                gated_ref[rows.start + n * CHUNK:rows.start + (n + 1) * CHUNK, cols] = (
                    u[toks, cols] * sp).astype(bf16)
        mixed.append(jnp.dot(gated_ref[rows, :], wout_ref[...], preferred_element_type=f32))
    for rows, m in zip(subs, mixed):
        y_ref[rows, :] = _layer_norm(ALPHA * x_ref[rows, :] + m, g_ref[...], b_ref[...])


def _sg_call(x, params, layer, g, b, ln_layer):
    t = x.shape[0]
    tile = lambda: pl.BlockSpec((SG_TILE, D_MODEL), lambda i: (i, 0))
    return pl.pallas_call(
        _sg_kernel,
        grid=(t // SG_TILE,),
        in_specs=[tile()] + [_layer_spec(p, layer) for p in params]
                 + [_layer_spec(g, ln_layer), _layer_spec(b, ln_layer)],
        out_specs=tile(),
        out_shape=jax.ShapeDtypeStruct((t, D_MODEL), f32),
        scratch_shapes=[pltpu.VMEM((SG_TILE, SG_WIDTH), bf16)],
        compiler_params=_params(48),
        name="sg_layer",
    )(x, *params, g, b)


def _ffn_kernel(*refs, with_proj):
    if with_proj:
        x_ref, a_ref, wp_ref, gm_ref, bm_ref, win_ref, wout_ref, g_ref, b_ref, y_ref, act_ref, xb_ref = refs
        src_ref = y_ref
    else:
        x_ref, win_ref, wout_ref, g_ref, b_ref, y_ref, act_ref, xb_ref = refs
        src_ref = x_ref
    subs = [slice(s * FFN_SUB, (s + 1) * FFN_SUB) for s in range(FFN_TILE // FFN_SUB)]
    n_chunks = D_FF // FF_CHUNK

    def prologue(rows):
        if with_proj:
            m = jnp.dot(a_ref[rows, :], wp_ref[...], preferred_element_type=f32)
            y_ref[rows, :] = _layer_norm(ALPHA * x_ref[rows, :] + m, gm_ref[...], bm_ref[...])
        xb_ref[rows, :] = src_ref[rows, :].astype(bf16)

    def chunk(rows, c):
        xb = xb_ref[rows, :]
        gate = jnp.dot(xb, win_ref[:, c * FF_CHUNK:(c + 1) * FF_CHUNK], preferred_element_type=f32)
        lin = jnp.dot(xb, win_ref[:, D_FF + c * FF_CHUNK:D_FF + (c + 1) * FF_CHUNK],
                      preferred_element_type=f32)
        act_ref[rows, c * FF_CHUNK:(c + 1) * FF_CHUNK] = (jax.nn.silu(gate) * lin).astype(bf16)

    def epilogue(rows):
        down = jnp.dot(act_ref[rows, :], wout_ref[...], preferred_element_type=f32)
        y_ref[rows, :] = _layer_norm(ALPHA * src_ref[rows, :] + down, g_ref[...], b_ref[...])

    prologue(subs[0])
    for s, rows in enumerate(subs):
        for c in range(n_chunks):
            chunk(rows, c)
            if c == n_chunks // 2 and s + 1 < len(subs):
                prologue(subs[s + 1])
        epilogue(rows)


def _ffn_call(x, params, layer, proj=None):
    t = x.shape[0]
    tile = lambda: pl.BlockSpec((FFN_TILE, D_MODEL), lambda i: (i, 0))
    in_specs, args = [tile()], [x]
    if proj is not None:
        attn, (wp, wp_layer), (gm, bm, ln_layer) = proj
        in_specs += [tile(), _layer_spec(wp, wp_layer), _layer_spec(gm, ln_layer), _layer_spec(bm, ln_layer)]
        args += [attn, wp, gm, bm]
    in_specs += [_layer_spec(p, layer) for p in params]
    args += list(params)
    return pl.pallas_call(
        functools.partial(_ffn_kernel, with_proj=proj is not None),
        grid=(t // FFN_TILE,),
        in_specs=in_specs,
        out_specs=tile(),
        out_shape=jax.ShapeDtypeStruct((t, D_MODEL), f32),
        scratch_shapes=[pltpu.VMEM((FFN_TILE, D_FF), bf16), pltpu.VMEM((FFN_TILE, D_MODEL), bf16)],
        compiler_params=_params(56),
        name="ffn_proj" if proj is not None else "ffn",
    )(*args)


def _trunk(x, n_img, w):
    t = x.shape[0]
    rows = t // (n_img * GRID_W)
    for i in range(DEPTH):
        j = i // 2
        if i % 2 == 0:
            qkv = _qkv_call(x, w["na_w_in"], j)
            attn = _na_call(qkv, w["na_bias"], j, n_img, rows)
            proj = (attn, (w["na_w_out"], j), (w["ln_mix_g"], w["ln_mix_b"], i))
        else:
            x = _sg_call(x, w["sg"], j, w["ln_mix_g"], w["ln_mix_b"], i)
            proj = None
        x = _ffn_call(x, w["ffn"], i, proj)
    return x


def kernel(x_prompt, x_sample, na_w_in, na_rpb, na_w_out, sg_w_in, sg_ln_g, sg_ln_b, sg_w_s, sg_b_s,
           sg_w_out, ln_mix_g, ln_mix_b, ffn_w_in, ffn_w_out, ln_ffn_g, ln_ffn_b):
    n_sg = sg_w_s.shape[0]
    group_dim = SG_WIDTH // SG_GROUPS
    vec = lambda a: a.reshape(a.shape[0], 1, a.shape[1])
    w = {
        "na_w_in": na_w_in.astype(bf16),
        "na_bias": _na_bias_tables(na_rpb),
        "na_w_out": na_w_out.astype(bf16),
        "sg": (
            sg_w_in.astype(bf16), vec(sg_ln_g), vec(sg_ln_b),
            sg_w_s.astype(bf16).reshape(n_sg, SG_GROUPS // 2, 2, CHUNK, CHUNK)
            .transpose(0, 1, 3, 2, 4).reshape(n_sg, SG_GROUPS // 2, CHUNK, 2 * CHUNK),
            jnp.repeat(sg_b_s.reshape(n_sg, SG_GROUPS // 2, 2, CHUNK).transpose(0, 1, 3, 2), group_dim, axis=-1),
            sg_w_out.astype(bf16)),
        "ln_mix_g": vec(ln_mix_g), "ln_mix_b": vec(ln_mix_b),
        "ffn": (ffn_w_in.astype(bf16), ffn_w_out.astype(bf16), vec(ln_ffn_g), vec(ln_ffn_b)),
    }
    b, s, d = x_prompt.shape
    y_prompt = _trunk(x_prompt.reshape(b * s, d), b, w).reshape(b, s, d)
    b, s, d = x_sample.shape
    y_sample = _trunk(x_sample.reshape(b * s, d), b, w).reshape(b, s, d)
    return (y_prompt, y_sample)
```

```python
import functools

import numpy as np
import jax
import jax.numpy as jnp
from jax import lax
from jax.experimental import pallas as pl
from jax.experimental.pallas import tpu as pltpu

D_MODEL = 1024
DEPTH = 4
GRID_W = 64
NA_HEADS = 16
HEAD_DIM = D_MODEL // NA_HEADS
MAX_KH = 8
KW = 16
CHUNK = 128
SG_GROUPS = 16
SG_WIDTH = D_MODEL
D_FF = 2816
ALPHA = (2 * DEPTH) ** 0.25
LN_EPS = 1e-5

LANES = 128
SUBLANES = 8
MXU_WIDTH = 256
GROUP_HEADS = MXU_WIDTH // HEAD_DIM
HEAD_GROUPS = D_MODEL // MXU_WIDTH
COL_HALF = GRID_W // 2
ROW_BLOCK = 32
KV_ROWS = ROW_BLOCK + MAX_KH
TOKEN_TILE = 1024
SG_TILE, SG_SUB = 1024, 256
FFN_TILE, FFN_SUB = 1024, 512
FF_CHUNK = 256
MASK_BIAS = -1e30
LOG2_E = 1.4426950408889634
MIB = 1024 * 1024

f32 = jnp.float32
bf16 = jnp.bfloat16


def _layer_norm(y, g, b):
    mu = jnp.mean(y, axis=-1, keepdims=True)
    yc = y - mu
    var = jnp.mean(yc * yc, axis=-1, keepdims=True)
    return yc * lax.rsqrt(var + LN_EPS) * g + b


def _gelu(z):
    return 0.5 * z * (1.0 + lax.erf(z * (0.5 ** 0.5)))


def _layer_spec(stacked, layer):
    zeros = (0,) * (stacked.ndim - 1)
    return pl.BlockSpec((None,) + stacked.shape[1:], lambda *_: (layer,) + zeros, pipeline_mode=pl.Buffered(1))


def _params(vmem_mib, n_grid=1):
    return pltpu.CompilerParams(
        dimension_semantics=("arbitrary",) * n_grid,
        vmem_limit_bytes=vmem_mib * MIB)


def _qkv_kernel(x_ref, w_ref, o_ref):
    xb = x_ref[...].astype(bf16)
    for c in range(3):
        cols = slice(c * D_MODEL, (c + 1) * D_MODEL)
        acc = jnp.dot(xb, w_ref[:, cols], preferred_element_type=f32)
        if c == 0:
            acc = acc * (HEAD_DIM ** -0.5 * LOG2_E)
        o_ref[:, cols] = acc.astype(bf16)


def _qkv_call(x, w, layer):
    t = x.shape[0]
    return pl.pallas_call(
        _qkv_kernel,
        grid=(t // TOKEN_TILE,),
        in_specs=[pl.BlockSpec((TOKEN_TILE, D_MODEL), lambda i: (i, 0)),
                  _layer_spec(w, layer)],
        out_specs=pl.BlockSpec((TOKEN_TILE, 3 * D_MODEL), lambda i: (i, 0)),
        out_shape=jax.ShapeDtypeStruct((t, 3 * D_MODEL), bf16),
        compiler_params=_params(40),
        name="na_qkv",
    )(x, w)


def _half_key_cols(ch):
    first, last = ch * COL_HALF, (ch + 1) * COL_HALF - 1
    start = lambda c: min(max(c - KW // 2, 0), GRID_W - KW)
    lo = start(first) // SUBLANES * SUBLANES
    hi = -(-(start(last) + KW) // SUBLANES) * SUBLANES
    return lo, hi - lo


def _na_kernel(q_ref, k_ref, v_ref, bias_ref, o_ref, *, rows):
    i = pl.program_id(1)
    wstart = jnp.clip(ROW_BLOCK * i - MAX_KH // 2, 0, rows - KV_ROWS)
    lane_head = lax.broadcasted_iota(jnp.int32, (COL_HALF, MXU_WIDTH), 1) // HEAD_DIM
    nkeys = MAX_KH * GRID_W
    for rr in range(ROW_BLOCK):
        r = ROW_BLOCK * i + rr
        rs = jnp.clip(r - MAX_KH // 2, 0, rows - MAX_KH)
        ks = pl.multiple_of((rs - wstart) * GRID_W, GRID_W)
        bs = pl.multiple_of((rs - r + MAX_KH - 1) * GRID_W, GRID_W)
        for hg in range(HEAD_GROUPS):
            cols = slice(hg * MXU_WIDTH, (hg + 1) * MXU_WIDTH)
            blocks = []
            for ch in range(2):
                qh = q_ref[rr * GRID_W + ch * COL_HALF:rr * GRID_W + (ch + 1) * COL_HALF, cols]
                blocks += [jnp.where(lane_head == h, qh, jnp.zeros_like(qh)) for h in range(GROUP_HEADS)]
            qm = jnp.concatenate(blocks, axis=0)
            kg = k_ref[pl.ds(ks, nkeys), cols]
            vg = v_ref[pl.ds(ks, nkeys), cols]
            s = lax.dot_general(kg, qm, (((1,), (1,)), ((), ())), preferred_element_type=f32)
            halves = []
            for ch in range(2):
                lo, n = _half_key_cols(ch)
                lanes = slice(ch * LANES, (ch + 1) * LANES)
                sc = jnp.concatenate(
                    [s[j * GRID_W + lo:j * GRID_W + lo + n, lanes]
                     + bias_ref[hg, pl.ds(bs + j * GRID_W + lo, n), lanes] for j in range(MAX_KH)], axis=0)
                m = jnp.max(sc, axis=0, keepdims=True)
                p = jnp.exp2(sc - m)
                pn = p * (1.0 / jnp.sum(p, axis=0, keepdims=True))
                pieces = []
                for j in range(MAX_KH):
                    pieces += [jnp.zeros((lo, LANES), f32), pn[j * n:(j + 1) * n],
                               jnp.zeros((GRID_W - lo - n, LANES), f32)]
                halves.append(jnp.concatenate([x for x in pieces if x.shape[0]], axis=0))
            pn = jnp.concatenate(halves, axis=1).astype(bf16)
            pv = lax.dot_general(pn, vg, (((0,), (0,)), ((), ())), preferred_element_type=f32)
            for ch in range(2):
                base = ch * LANES
                out = pv[base:base + COL_HALF]
                for h in range(1, GROUP_HEADS):
                    out = jnp.where(lane_head == h, pv[base + h * COL_HALF:base + (h + 1) * COL_HALF], out)
                o_ref[rr * GRID_W + ch * COL_HALF:rr * GRID_W + (ch + 1) * COL_HALF, cols] = out.astype(bf16)


def _na_call(qkv, bias, layer, n_img, rows):
    t = qkv.shape[0]
    nb = rows // ROW_BLOCK
    blk = ROW_BLOCK * GRID_W
    win = KV_ROWS * GRID_W

    def kv_map(col):
        def index_map(b, i):
            wstart = jnp.clip(ROW_BLOCK * i - MAX_KH // 2, 0, rows - KV_ROWS)
            return ((b * rows + wstart) * GRID_W, col * D_MODEL)
        return index_map

    kv_spec = lambda col: pl.BlockSpec((pl.Element(win), pl.Element(D_MODEL)), kv_map(col))
    return pl.pallas_call(
        functools.partial(_na_kernel, rows=rows),
        grid=(n_img, nb),
        in_specs=[pl.BlockSpec((blk, D_MODEL), lambda b, i: (b * nb + i, 0)),
                  kv_spec(1), kv_spec(2),
                  _layer_spec(bias, layer)],
        out_specs=pl.BlockSpec((blk, D_MODEL), lambda b, i: (b * nb + i, 0)),
        out_shape=jax.ShapeDtypeStruct((t, D_MODEL), bf16),
        compiler_params=_params(48, 2),
        name="na_attn",
    )(qkv, qkv, qkv, bias)


def _na_bias_tables(rpb):
    n_layers = rpb.shape[0]
    n_dr, n_dc = 2 * MAX_KH - 1, 2 * KW - 1
    c = (np.arange(2)[:, None, None] * COL_HALF + np.arange(COL_HALF)[None, None, :]
         + np.zeros((1, GROUP_HEADS, 1), np.int64)).reshape(-1)
    hh = (np.zeros((2, 1, COL_HALF), np.int64) + np.arange(GROUP_HEADS)[None, :, None]).reshape(-1)
    col_start = np.clip(c - KW // 2, 0, GRID_W - KW)
    kc = np.arange(GRID_W)
    in_win = (kc[:, None] >= col_start[None, :]) & (kc[:, None] < col_start[None, :] + KW)
    e = np.arange(n_dc)
    select = ((e[:, None, None, None] == (kc[:, None] - c[None, :] + KW - 1)[None, None])
              & (np.arange(GROUP_HEADS)[None, :, None, None] == hh[None, None, None, :]) & in_win[None, None])
    select = select.reshape(n_dc * GROUP_HEADS, GRID_W * MXU_WIDTH).astype(np.float32)
    lhs = rpb.astype(f32).reshape(n_layers, HEAD_GROUPS, GROUP_HEADS, n_dr, n_dc)
    lhs = lhs.transpose(0, 1, 3, 4, 2).reshape(n_layers * HEAD_GROUPS * n_dr, n_dc * GROUP_HEADS)
    t = jnp.dot(lhs, select, precision=lax.Precision.HIGHEST)
    t = t.reshape(n_layers, HEAD_GROUPS, n_dr, GRID_W, MXU_WIDTH)
    t = jnp.where(in_win[None, None, None], t * LOG2_E, MASK_BIAS)
    return t.reshape(n_layers, HEAD_GROUPS, n_dr * GRID_W, MXU_WIDTH)


def _sg_kernel(x_ref, win_ref, lng_ref, lnb_ref, wsp_ref, bsp_ref, wout_ref, g_ref, b_ref,
               y_ref, gated_ref):
    lane = lax.broadcasted_iota(jnp.int32, (CHUNK, LANES), 1)
    first_group = lane < SG_WIDTH // SG_GROUPS
    zero = jnp.zeros((CHUNK, LANES), bf16)
    subs = [slice(s * SG_SUB, (s + 1) * SG_SUB) for s in range(SG_TILE // SG_SUB)]
    raw = []
    for rows in subs:
        xb = x_ref[rows, :].astype(bf16)
        raw.append((jnp.dot(xb, win_ref[:, :SG_WIDTH], preferred_element_type=f32),
                    jnp.dot(xb, win_ref[:, SG_WIDTH:], preferred_element_type=f32)))
    act = [(_gelu(u), _layer_norm(_gelu(v), lng_ref[...], lnb_ref[...]).astype(bf16)) for u, v in raw]
    mixed = []
    for rows, (u, vb) in zip(subs, act):
        for n in range(SG_SUB // CHUNK):
            toks = slice(n * CHUNK, (n + 1) * CHUNK)
            for j in range(SG_GROUPS // 2):
                cols = slice(j * LANES, (j + 1) * LANES)
                vp = vb[toks, cols]
                rhs = jnp.concatenate([jnp.where(first_group, vp, zero),
                                       jnp.where(first_group, zero, vp)], axis=0)
                sp = jnp.dot(wsp_ref[j], rhs, preferred_element_type=f32) + bsp_ref[j]
                gated_ref[rows.start + n * CHUNK:rows.start + (n + 1) * CHUNK, cols] = (
                    u[toks, cols] * sp).astype(bf16)
        mixed.append(jnp.dot(gated_ref[rows, :], wout_ref[...], preferred_element_type=f32))
    for rows, m in zip(subs, mixed):
        y_ref[rows, :] = _layer_norm(ALPHA * x_ref[rows, :] + m, g_ref[...], b_ref[...])


def _sg_call(x, params, layer, g, b, ln_layer):
    t = x.shape[0]
    tile = lambda: pl.BlockSpec((SG_TILE, D_MODEL), lambda i: (i, 0))
    return pl.pallas_call(
        _sg_kernel,
        grid=(t // SG_TILE,),
        in_specs=[tile()] + [_layer_spec(p, layer) for p in params]
                 + [_layer_spec(g, ln_layer), _layer_spec(b, ln_layer)],
        out_specs=tile(),
        out_shape=jax.ShapeDtypeStruct((t, D_MODEL), f32),
        scratch_shapes=[pltpu.VMEM((SG_TILE, SG_WIDTH), bf16)],
        compiler_params=_params(48),
        name="sg_layer",
    )(x, *params, g, b)


def _ffn_kernel(*refs, with_proj):
    if with_proj:
        x_ref, a_ref, wp_ref, gm_ref, bm_ref, win_ref, wout_ref, g_ref, b_ref, y_ref, act_ref, xb_ref = refs
        src_ref = y_ref
    else:
        x_ref, win_ref, wout_ref, g_ref, b_ref, y_ref, act_ref, xb_ref = refs
        src_ref = x_ref
    subs = [slice(s * FFN_SUB, (s + 1) * FFN_SUB) for s in range(FFN_TILE // FFN_SUB)]
    n_chunks = D_FF // FF_CHUNK

    def prologue(rows):
        if with_proj:
            m = jnp.dot(a_ref[rows, :], wp_ref[...], preferred_element_type=f32)
            y_ref[rows, :] = _layer_norm(ALPHA * x_ref[rows, :] + m, gm_ref[...], bm_ref[...])
        xb_ref[rows, :] = src_ref[rows, :].astype(bf16)

    def chunk(rows, c):
        xb = xb_ref[rows, :]
        gate = jnp.dot(xb, win_ref[:, c * FF_CHUNK:(c + 1) * FF_CHUNK], preferred_element_type=f32)
        lin = jnp.dot(xb, win_ref[:, D_FF + c * FF_CHUNK:D_FF + (c + 1) * FF_CHUNK],
                      preferred_element_type=f32)
        act_ref[rows, c * FF_CHUNK:(c + 1) * FF_CHUNK] = (jax.nn.silu(gate) * lin).astype(bf16)

    def epilogue(rows):
        half = (rows.stop - rows.start) // 2
        for part in (slice(rows.start, rows.start + half), slice(rows.start + half, rows.stop)):
            down = jnp.dot(act_ref[part, :], wout_ref[...], preferred_element_type=f32)
            y_ref[part, :] = _layer_norm(ALPHA * src_ref[part, :] + down, g_ref[...], b_ref[...])

    prologue(subs[0])
    for s, rows in enumerate(subs):
        for c in range(n_chunks):
            chunk(rows, c)
            if c == n_chunks // 2 and s + 1 < len(subs):
                prologue(subs[s + 1])
        epilogue(rows)


def _ffn_call(x, params, layer, proj=None):
    t = x.shape[0]
    tile = lambda: pl.BlockSpec((FFN_TILE, D_MODEL), lambda i: (i, 0))
    in_specs, args = [tile()], [x]
    if proj is not None:
        attn, (wp, wp_layer), (gm, bm, ln_layer) = proj
        in_specs += [tile(), _layer_spec(wp, wp_layer), _layer_spec(gm, ln_layer), _layer_spec(bm, ln_layer)]
        args += [attn, wp, gm, bm]
    in_specs += [_layer_spec(p, layer) for p in params]
    args += list(params)
    return pl.pallas_call(
        functools.partial(_ffn_kernel, with_proj=proj is not None),
        grid=(t // FFN_TILE,),
        in_specs=in_specs,
        out_specs=tile(),
        out_shape=jax.ShapeDtypeStruct((t, D_MODEL), f32),
        scratch_shapes=[pltpu.VMEM((FFN_TILE, D_FF), bf16), pltpu.VMEM((FFN_TILE, D_MODEL), bf16)],
        compiler_params=_params(56),
        name="ffn_proj" if proj is not None else "ffn",
    )(*args)


def _trunk(x, n_img, w):
    t = x.shape[0]
    rows = t // (n_img * GRID_W)
    for i in range(DEPTH):
        j = i // 2
        if i % 2 == 0:
            qkv = _qkv_call(x, w["na_w_in"], j)
            attn = _na_call(qkv, w["na_bias"], j, n_img, rows)
            proj = (attn, (w["na_w_out"], j), (w["ln_mix_g"], w["ln_mix_b"], i))
        else:
            x = _sg_call(x, w["sg"], j, w["ln_mix_g"], w["ln_mix_b"], i)
            proj = None
        x = _ffn_call(x, w["ffn"], i, proj)
    return x


def kernel(x_prompt, x_sample, na_w_in, na_rpb, na_w_out, sg_w_in, sg_ln_g, sg_ln_b, sg_w_s, sg_b_s,
           sg_w_out, ln_mix_g, ln_mix_b, ffn_w_in, ffn_w_out, ln_ffn_g, ln_ffn_b):
    n_sg = sg_w_s.shape[0]
    group_dim = SG_WIDTH // SG_GROUPS
    vec = lambda a: a.reshape(a.shape[0], 1, a.shape[1])
    w = {
        "na_w_in": na_w_in.astype(bf16),
        "na_bias": _na_bias_tables(na_rpb),
        "na_w_out": na_w_out.astype(bf16),
        "sg": (
            sg_w_in.astype(bf16), vec(sg_ln_g), vec(sg_ln_b),
            sg_w_s.astype(bf16).reshape(n_sg, SG_GROUPS // 2, 2, CHUNK, CHUNK)
            .transpose(0, 1, 3, 2, 4).reshape(n_sg, SG_GROUPS // 2, CHUNK, 2 * CHUNK),
            jnp.repeat(sg_b_s.reshape(n_sg, SG_GROUPS // 2, 2, CHUNK).transpose(0, 1, 3, 2), group_dim, axis=-1),
            sg_w_out.astype(bf16)),
        "ln_mix_g": vec(ln_mix_g), "ln_mix_b": vec(ln_mix_b),
        "ffn": (ffn_w_in.astype(bf16), ffn_w_out.astype(bf16), vec(ln_ffn_g), vec(ln_ffn_b)),
    }
    b, s, d = x_prompt.shape
    y_prompt = _trunk(x_prompt.reshape(b * s, d), b, w).reshape(b, s, d)
    b, s, d = x_sample.shape
    y_sample = _trunk(x_sample.reshape(b * s, d), b, w).reshape(b, s, d)
    return (y_prompt, y_sample)
```

```python
import functools

import numpy as np
import jax
import jax.numpy as jnp
from jax import lax
from jax.experimental import pallas as pl
from jax.experimental.pallas import tpu as pltpu

D_MODEL = 1024
DEPTH = 4
GRID_W = 64
NA_HEADS = 16
HEAD_DIM = D_MODEL // NA_HEADS
MAX_KH = 8
KW = 16
CHUNK = 128
SG_GROUPS = 16
SG_WIDTH = D_MODEL
D_FF = 2816
ALPHA = (2 * DEPTH) ** 0.25
LN_EPS = 1e-5

LANES = 128
SUBLANES = 8
MXU_WIDTH = 256
GROUP_HEADS = MXU_WIDTH // HEAD_DIM
HEAD_GROUPS = D_MODEL // MXU_WIDTH
COL_HALF = GRID_W // 2
ROW_BLOCK = 16
KV_ROWS = ROW_BLOCK + MAX_KH
TOKEN_TILE = 1024
SG_TILE, SG_SUB = 1024, 256
FFN_TILE, FFN_SUB = 1024, 512
FF_CHUNK = 256
MASK_BIAS = -1e30
LOG2_E = 1.4426950408889634
MIB = 1024 * 1024

f32 = jnp.float32
bf16 = jnp.bfloat16


def _layer_norm(y, g, b):
    mu = jnp.mean(y, axis=-1, keepdims=True)
    yc = y - mu
    var = jnp.mean(yc * yc, axis=-1, keepdims=True)
    return yc * lax.rsqrt(var + LN_EPS) * g + b


def _gelu(z):
    return 0.5 * z * (1.0 + lax.erf(z * (0.5 ** 0.5)))


def _layer_spec(stacked, layer):
    zeros = (0,) * (stacked.ndim - 1)
    return pl.BlockSpec((None,) + stacked.shape[1:], lambda *_: (layer,) + zeros, pipeline_mode=pl.Buffered(1))


def _params(vmem_mib, n_grid=1):
    return pltpu.CompilerParams(
        dimension_semantics=("arbitrary",) * n_grid,
        vmem_limit_bytes=vmem_mib * MIB)


def _qkv_kernel(x_ref, w_ref, o_ref):
    xb = x_ref[...].astype(bf16)
    for c in range(3):
        cols = slice(c * D_MODEL, (c + 1) * D_MODEL)
        acc = jnp.dot(xb, w_ref[:, cols], preferred_element_type=f32)
        if c == 0:
            acc = acc * (HEAD_DIM ** -0.5 * LOG2_E)
        o_ref[:, cols] = acc.astype(bf16)


def _qkv_call(x, w, layer):
    t = x.shape[0]
    return pl.pallas_call(
        _qkv_kernel,
        grid=(t // TOKEN_TILE,),
        in_specs=[pl.BlockSpec((TOKEN_TILE, D_MODEL), lambda i: (i, 0)),
                  _layer_spec(w, layer)],
        out_specs=pl.BlockSpec((TOKEN_TILE, 3 * D_MODEL), lambda i: (i, 0)),
        out_shape=jax.ShapeDtypeStruct((t, 3 * D_MODEL), bf16),
        compiler_params=_params(40),
        name="na_qkv",
    )(x, w)


def _half_key_cols(ch):
    first, last = ch * COL_HALF, (ch + 1) * COL_HALF - 1
    start = lambda c: min(max(c - KW // 2, 0), GRID_W - KW)
    lo = start(first) // SUBLANES * SUBLANES
    hi = -(-(start(last) + KW) // SUBLANES) * SUBLANES
    return lo, hi - lo


def _na_kernel(q_ref, k_ref, v_ref, bias_ref, o_ref, *, rows):
    i = pl.program_id(1)
    wstart = jnp.clip(ROW_BLOCK * i - MAX_KH // 2, 0, rows - KV_ROWS)
    lane_head = lax.broadcasted_iota(jnp.int32, (COL_HALF, MXU_WIDTH), 1) // HEAD_DIM
    nkeys = MAX_KH * GRID_W
    for rr in range(ROW_BLOCK):
        r = ROW_BLOCK * i + rr
        rs = jnp.clip(r - MAX_KH // 2, 0, rows - MAX_KH)
        ks = pl.multiple_of((rs - wstart) * GRID_W, GRID_W)
        bs = pl.multiple_of((rs - r + MAX_KH - 1) * GRID_W, GRID_W)
        for hg in range(HEAD_GROUPS):
            cols = slice(hg * MXU_WIDTH, (hg + 1) * MXU_WIDTH)
            blocks = []
            for ch in range(2):
                qh = q_ref[rr * GRID_W + ch * COL_HALF:rr * GRID_W + (ch + 1) * COL_HALF, cols]
                blocks += [jnp.where(lane_head == h, qh, jnp.zeros_like(qh)) for h in range(GROUP_HEADS)]
            qm = jnp.concatenate(blocks, axis=0)
            kg = k_ref[pl.ds(ks, nkeys), cols]
            vg = v_ref[pl.ds(ks, nkeys), cols]
            s = lax.dot_general(kg, qm, (((1,), (1,)), ((), ())), preferred_element_type=f32)
            halves = []
            for ch in range(2):
                lo, n = _half_key_cols(ch)
                lanes = slice(ch * LANES, (ch + 1) * LANES)
                sc = jnp.concatenate(
                    [s[j * GRID_W + lo:j * GRID_W + lo + n, lanes]
                     + bias_ref[hg, pl.ds(bs + j * GRID_W + lo, n), lanes] for j in range(MAX_KH)], axis=0)
                m = jnp.max(sc, axis=0, keepdims=True)
                p = jnp.exp2(sc - m)
                pn = p * (1.0 / jnp.sum(p, axis=0, keepdims=True))
                pieces = []
                for j in range(MAX_KH):
                    pieces += [jnp.zeros((lo, LANES), f32), pn[j * n:(j + 1) * n],
                               jnp.zeros((GRID_W - lo - n, LANES), f32)]
                halves.append(jnp.concatenate([x for x in pieces if x.shape[0]], axis=0))
            pn = jnp.concatenate(halves, axis=1).astype(bf16)
            pv = lax.dot_general(pn, vg, (((0,), (0,)), ((), ())), preferred_element_type=f32)
            for ch in range(2):
                base = ch * LANES
                out = pv[base:base + COL_HALF]
                for h in range(1, GROUP_HEADS):
                    out = jnp.where(lane_head == h, pv[base + h * COL_HALF:base + (h + 1) * COL_HALF], out)
                o_ref[rr * GRID_W + ch * COL_HALF:rr * GRID_W + (ch + 1) * COL_HALF, cols] = out.astype(bf16)


def _na_call(qkv, bias, layer, n_img, rows):
    t = qkv.shape[0]
    nb = rows // ROW_BLOCK
    blk = ROW_BLOCK * GRID_W
    win = KV_ROWS * GRID_W

    def kv_map(col):
        def index_map(b, i):
            wstart = jnp.clip(ROW_BLOCK * i - MAX_KH // 2, 0, rows - KV_ROWS)
            return ((b * rows + wstart) * GRID_W, col * D_MODEL)
        return index_map

    kv_spec = lambda col: pl.BlockSpec((pl.Element(win), pl.Element(D_MODEL)), kv_map(col))
    return pl.pallas_call(
        functools.partial(_na_kernel, rows=rows),
        grid=(n_img, nb),
        in_specs=[pl.BlockSpec((blk, D_MODEL), lambda b, i: (b * nb + i, 0)),
                  kv_spec(1), kv_spec(2),
                  _layer_spec(bias, layer)],
        out_specs=pl.BlockSpec((blk, D_MODEL), lambda b, i: (b * nb + i, 0)),
        out_shape=jax.ShapeDtypeStruct((t, D_MODEL), bf16),
        compiler_params=_params(48, 2),
        name="na_attn",
    )(qkv, qkv, qkv, bias)


def _na_bias_tables(rpb):
    n_layers = rpb.shape[0]
    n_dr, n_dc = 2 * MAX_KH - 1, 2 * KW - 1
    c = (np.arange(2)[:, None, None] * COL_HALF + np.arange(COL_HALF)[None, None, :]
         + np.zeros((1, GROUP_HEADS, 1), np.int64)).reshape(-1)
    hh = (np.zeros((2, 1, COL_HALF), np.int64) + np.arange(GROUP_HEADS)[None, :, None]).reshape(-1)
    col_start = np.clip(c - KW // 2, 0, GRID_W - KW)
    kc = np.arange(GRID_W)
    in_win = (kc[:, None] >= col_start[None, :]) & (kc[:, None] < col_start[None, :] + KW)
    e = np.arange(n_dc)
    select = ((e[:, None, None, None] == (kc[:, None] - c[None, :] + KW - 1)[None, None])
              & (np.arange(GROUP_HEADS)[None, :, None, None] == hh[None, None, None, :]) & in_win[None, None])
    select = select.reshape(n_dc * GROUP_HEADS, GRID_W * MXU_WIDTH).astype(np.float32)
    lhs = rpb.astype(f32).reshape(n_layers, HEAD_GROUPS, GROUP_HEADS, n_dr, n_dc)
    lhs = lhs.transpose(0, 1, 3, 4, 2).reshape(n_layers * HEAD_GROUPS * n_dr, n_dc * GROUP_HEADS)
    t = jnp.dot(lhs, select, precision=lax.Precision.HIGHEST)
    t = t.reshape(n_layers, HEAD_GROUPS, n_dr, GRID_W, MXU_WIDTH)
    t = jnp.where(in_win[None, None, None], t * LOG2_E, MASK_BIAS)
    return t.reshape(n_layers, HEAD_GROUPS, n_dr * GRID_W, MXU_WIDTH)


def _sg_kernel(x_ref, win_ref, lng_ref, lnb_ref, wsp_ref, bsp_ref, wout_ref, g_ref, b_ref,
               y_ref, gated_ref):
    lane = lax.broadcasted_iota(jnp.int32, (CHUNK, LANES), 1)
    first_group = lane < SG_WIDTH // SG_GROUPS
    zero = jnp.zeros((CHUNK, LANES), bf16)
    subs = [slice(s * SG_SUB, (s + 1) * SG_SUB) for s in range(SG_TILE // SG_SUB)]
    raw = []
    for rows in subs:
        xb = x_ref[rows, :].astype(bf16)
        raw.append((jnp.dot(xb, win_ref[:, :SG_WIDTH], preferred_element_type=f32),
                    jnp.dot(xb, win_ref[:, SG_WIDTH:], preferred_element_type=f32)))
    act = [(_gelu(u), _layer_norm(_gelu(v), lng_ref[...], lnb_ref[...]).astype(bf16)) for u, v in raw]
    mixed = []
    for rows, (u, vb) in zip(subs, act):
        for n in range(SG_SUB // CHUNK):
            toks = slice(n * CHUNK, (n + 1) * CHUNK)
            for j in range(SG_GROUPS // 2):
                cols = slice(j * LANES, (j + 1) * LANES)
                vp = vb[toks, cols]
                rhs = jnp.concatenate([jnp.where(first_group, vp, zero),
                                       jnp.where(first_group, zero, vp)], axis=0)
                sp = jnp.dot(wsp_ref[j], rhs, preferred_element_type=f32) + bsp_ref[j]
                gated_ref[rows.start + n * CHUNK:rows.start + (n + 1) * CHUNK, cols] = (
                    u[toks, cols] * sp).astype(bf16)
        mixed.append(jnp.dot(gated_ref[rows, :], wout_ref[...], preferred_element_type=f32))
    for rows, m in zip(subs, mixed):
        y_ref[rows, :] = _layer_norm(ALPHA * x_ref[rows, :] + m, g_ref[...], b_ref[...])


def _sg_call(x, params, layer, g, b, ln_layer):
    t = x.shape[0]
    tile = lambda: pl.BlockSpec((SG_TILE, D_MODEL), lambda i: (i, 0))
    return pl.pallas_call(
        _sg_kernel,
        grid=(t // SG_TILE,),
        in_specs=[tile()] + [_layer_spec(p, layer) for p in params]
                 + [_layer_spec(g, ln_layer), _layer_spec(b, ln_layer)],
        out_specs=tile(),
        out_shape=jax.ShapeDtypeStruct((t, D_MODEL), f32),
        scratch_shapes=[pltpu.VMEM((SG_TILE, SG_WIDTH), bf16)],
        compiler_params=_params(48),
        name="sg_layer",
    )(x, *params, g, b)


def _ffn_kernel(*refs, with_proj):
    if with_proj:
        x_ref, a_ref, wp_ref, gm_ref, bm_ref, win_ref, wout_ref, g_ref, b_ref, y_ref, act_ref, xb_ref = refs
        src_ref = y_ref
    else:
        x_ref, win_ref, wout_ref, g_ref, b_ref, y_ref, act_ref, xb_ref = refs
        src_ref = x_ref
    subs = [slice(s * FFN_SUB, (s + 1) * FFN_SUB) for s in range(FFN_TILE // FFN_SUB)]
    n_chunks = D_FF // FF_CHUNK

    def prologue(rows):
        if with_proj:
            m = jnp.dot(a_ref[rows, :], wp_ref[...], preferred_element_type=f32)
            y_ref[rows, :] = _layer_norm(ALPHA * x_ref[rows, :] + m, gm_ref[...], bm_ref[...])
        xb_ref[rows, :] = src_ref[rows, :].astype(bf16)

    def halves(rows):
        mid = (rows.start + rows.stop) // 2
        return slice(rows.start, mid), slice(mid, rows.stop)

    def chunk(rows, c):
        xb = xb_ref[rows, :]
        gate = jnp.dot(xb, win_ref[:, c * FF_CHUNK:(c + 1) * FF_CHUNK], preferred_element_type=f32)
        lin = jnp.dot(xb, win_ref[:, D_FF + c * FF_CHUNK:D_FF + (c + 1) * FF_CHUNK],
                      preferred_element_type=f32)
        act_ref[rows, c * FF_CHUNK:(c + 1) * FF_CHUNK] = (jax.nn.silu(gate) * lin).astype(bf16)

    def epilogue(rows):
        for part in halves(rows):
            down = jnp.dot(act_ref[part, :], wout_ref[...], preferred_element_type=f32)
            y_ref[part, :] = _layer_norm(ALPHA * src_ref[part, :] + down, g_ref[...], b_ref[...])

    for part in halves(subs[0]):
        prologue(part)
    for s, rows in enumerate(subs):
        for c in range(n_chunks):
            if s == 0 and c == 0 and with_proj:
                for part in halves(rows):
                    chunk(part, c)
            else:
                chunk(rows, c)
            if c == n_chunks // 2 and s + 1 < len(subs):
                prologue(subs[s + 1])
        epilogue(rows)


def _ffn_call(x, params, layer, proj=None):
    t = x.shape[0]
    tile = lambda: pl.BlockSpec((FFN_TILE, D_MODEL), lambda i: (i, 0))
    in_specs, args = [tile()], [x]
    if proj is not None:
        attn, (wp, wp_layer), (gm, bm, ln_layer) = proj
        in_specs += [tile(), _layer_spec(wp, wp_layer), _layer_spec(gm, ln_layer), _layer_spec(bm, ln_layer)]
        args += [attn, wp, gm, bm]
    in_specs += [_layer_spec(p, layer) for p in params]
    args += list(params)
    return pl.pallas_call(
        functools.partial(_ffn_kernel, with_proj=proj is not None),
        grid=(t // FFN_TILE,),
        in_specs=in_specs,
        out_specs=tile(),
        out_shape=jax.ShapeDtypeStruct((t, D_MODEL), f32),
        scratch_shapes=[pltpu.VMEM((FFN_TILE, D_FF), bf16), pltpu.VMEM((FFN_TILE, D_MODEL), bf16)],
        compiler_params=_params(56),
        name="ffn_proj" if proj is not None else "ffn",
    )(*args)


def _trunk(x, n_img, w):
    t = x.shape[0]
    rows = t // (n_img * GRID_W)
    for i in range(DEPTH):
        j = i // 2
        if i % 2 == 0:
            qkv = _qkv_call(x, w["na_w_in"], j)
            attn = _na_call(qkv, w["na_bias"], j, n_img, rows)
            proj = (attn, (w["na_w_out"], j), (w["ln_mix_g"], w["ln_mix_b"], i))
        else:
            x = _sg_call(x, w["sg"], j, w["ln_mix_g"], w["ln_mix_b"], i)
            proj = None
        x = _ffn_call(x, w["ffn"], i, proj)
    return x


def kernel(x_prompt, x_sample, na_w_in, na_rpb, na_w_out, sg_w_in, sg_ln_g, sg_ln_b, sg_w_s, sg_b_s,
           sg_w_out, ln_mix_g, ln_mix_b, ffn_w_in, ffn_w_out, ln_ffn_g, ln_ffn_b):
    n_sg = sg_w_s.shape[0]
    group_dim = SG_WIDTH // SG_GROUPS
    vec = lambda a: a.reshape(a.shape[0], 1, a.shape[1])
    w = {
        "na_w_in": na_w_in.astype(bf16),
        "na_bias": _na_bias_tables(na_rpb),
        "na_w_out": na_w_out.astype(bf16),
        "sg": (
            sg_w_in.astype(bf16), vec(sg_ln_g), vec(sg_ln_b),
            sg_w_s.astype(bf16).reshape(n_sg, SG_GROUPS // 2, 2, CHUNK, CHUNK)
            .transpose(0, 1, 3, 2, 4).reshape(n_sg, SG_GROUPS // 2, CHUNK, 2 * CHUNK),
            jnp.repeat(sg_b_s.reshape(n_sg, SG_GROUPS // 2, 2, CHUNK).transpose(0, 1, 3, 2), group_dim, axis=-1),
            sg_w_out.astype(bf16)),
        "ln_mix_g": vec(ln_mix_g), "ln_mix_b": vec(ln_mix_b),
        "ffn": (ffn_w_in.astype(bf16), ffn_w_out.astype(bf16), vec(ln_ffn_g), vec(ln_ffn_b)),
    }
    b, s, d = x_prompt.shape
    y_prompt = _trunk(x_prompt.reshape(b * s, d), b, w).reshape(b, s, d)
    b, s, d = x_sample.shape
    y_sample = _trunk(x_sample.reshape(b * s, d), b, w).reshape(b, s, d)
    return (y_prompt, y_sample)
```
